```python
import math
import jax, jax.numpy as jnp
from jax import lax
import numpy as np

D_MODEL = 1024
BATCH = 32
SEQ = 2048
DEPTH = 4
DEC_BATCH = 8
DEC_SEQ = 2048
PAST_LEN = 128

N_MIXERS = 4
GRID_W = 64
Q_BLOCK = 128
NORM_EPS = 1e-6
A_HEADS = 16
A_KV_HEADS = 4
A_HEAD_DIM = D_MODEL // A_HEADS
A_GROUP = A_HEADS // A_KV_HEADS
ROPE_THETA = 10000.0
POOL_WINDOWS = (2, 4, 8, 16)
POOL_GROUP = D_MODEL // len(POOL_WINDOWS)
C_HEADS = 8
C_HEAD_DIM = D_MODEL // (2 * C_HEADS)
F_GROUPS = 4
F_GROUP_DIM = D_MODEL // F_GROUPS
D_FF = -(-8 * D_MODEL // (3 * 256)) * 256
N_A = len(range(0, DEPTH, N_MIXERS))
N_B = len(range(1, DEPTH, N_MIXERS))
N_C = len(range(2, DEPTH, N_MIXERS))
N_D = len(range(3, DEPTH, N_MIXERS))

kernel_name = "hybrid_interleaved_bidir_encoder"

F32 = jnp.float32


def rmsnorm(x, gain):
    xf = x.astype(F32)
    y = xf * lax.rsqrt(jnp.mean(xf * xf, axis=-1, keepdims=True) + NORM_EPS)
    return (y * gain.astype(F32)).astype(x.dtype)


def axial_rope_angles(T):
    rows = T // GRID_W
    r, c = jnp.meshgrid(jnp.arange(rows), jnp.arange(GRID_W), indexing="ij")
    r = r.reshape(-1).astype(F32)
    c = c.reshape(-1).astype(F32)
    n = A_HEAD_DIM // 4
    freqs = ROPE_THETA ** (-jnp.arange(n, dtype=F32) / n)
    ang = jnp.concatenate([r[:, None] * freqs, c[:, None] * freqs], axis=-1)
    return jnp.cos(ang), jnp.sin(ang)


def apply_rope(x, cos, sin):
    xp = x.astype(F32).reshape(x.shape[:-1] + (x.shape[-1] // 2, 2))
    xe, xo = xp[..., 0], xp[..., 1]
    out = jnp.stack([xe * cos - xo * sin, xe * sin + xo * cos], axis=-1)
    return out.reshape(x.shape).astype(x.dtype)


def mixer_gqa_axial(h, w_qkv, q_gain, k_gain, w_o):
    B, T, _ = h.shape
    nblk = T // Q_BLOCK
    qkv = h @ w_qkv
    q, k, v = jnp.split(qkv, [A_HEADS * A_HEAD_DIM, (A_HEADS + A_KV_HEADS) * A_HEAD_DIM], axis=-1)
    q = rmsnorm(q.reshape(B, T, A_KV_HEADS, A_GROUP, A_HEAD_DIM), q_gain)
    k = rmsnorm(k.reshape(B, T, A_KV_HEADS, A_HEAD_DIM), k_gain)
    v = v.reshape(B, T, A_KV_HEADS, A_HEAD_DIM)
    cos, sin = axial_rope_angles(T)
    q = apply_rope(q, cos[:, None, None], sin[:, None, None]) * (A_HEAD_DIM ** -0.5)
    k = apply_rope(k, cos[:, None], sin[:, None])
    qb = q.reshape(B, nblk, Q_BLOCK, A_KV_HEADS, A_GROUP, A_HEAD_DIM).swapaxes(0, 1)

    def attend(qi):
        s = jnp.einsum("bqkgd,bskd->bkgqs", qi, k, preferred_element_type=F32)
        p = jax.nn.softmax(s, axis=-1).astype(v.dtype)
        return jnp.einsum("bkgqs,bskd->bqkgd", p, v)

    o = lax.map(attend, qb).swapaxes(0, 1).reshape(B, T, A_HEADS * A_HEAD_DIM)
    return o @ w_o


def mixer_pool(h, w_pool, scale):
    B, T, D = h.shape
    hf = h.astype(F32)
    csum = jnp.concatenate([jnp.zeros((B, 1, D), F32), lax.cumsum(hf, axis=1)], axis=1)
    t = jnp.arange(T)
    parts = []
    for g, win in enumerate(POOL_WINDOWS):
        sl = slice(g * POOL_GROUP, (g + 1) * POOL_GROUP)
        lo = jnp.clip(t - win // 2, 0, T - 1)
        hi = jnp.clip(t + win // 2 - 1, 0, T - 1)
        cg = csum[..., sl]
        cnt = (hi - lo + 1).astype(F32)[:, None]
        mean = (jnp.take(cg, hi + 1, axis=1) - jnp.take(cg, lo, axis=1)) / cnt
        parts.append(mean - hf[..., sl])
    p = jnp.stack(parts, axis=2).astype(h.dtype)
    y = jnp.einsum("btgc,gce->btge", p, w_pool).reshape(B, T, D)
    return y * scale


def mixer_diff_attn(h, w_qkv, lam_params, sub_gain, w_o, lam_init):
    B, T, _ = h.shape
    nblk = T // Q_BLOCK
    width = C_HEADS * 2 * C_HEAD_DIM
    q, k, v = jnp.split(h @ w_qkv, [width, 2 * width], axis=-1)
    q = q.reshape(B, T, C_HEADS, 2, C_HEAD_DIM) * (C_HEAD_DIM ** -0.5)
    k = k.reshape(B, T, C_HEADS, 2, C_HEAD_DIM)
    v = v.reshape(B, T, C_HEADS, 2 * C_HEAD_DIM)
    lp = lam_params.astype(F32)
    lam = jnp.exp(jnp.sum(lp[0] * lp[1])) - jnp.exp(jnp.sum(lp[2] * lp[3])) + lam_init
    slopes = 2.0 ** (-(8.0 / C_HEADS) * jnp.arange(1, C_HEADS + 1, dtype=F32))
    pos = jnp.arange(T)
    qb = q.reshape(B, nblk, Q_BLOCK, C_HEADS, 2, C_HEAD_DIM).swapaxes(0, 1)
    tb = pos.reshape(nblk, Q_BLOCK)

    def attend(args):
        qi, tq = args
        s = jnp.einsum("bqhcd,bshcd->bhcqs", qi, k, preferred_element_type=F32)
        dist = jnp.abs(tq[:, None] - pos[None, :]).astype(F32)
        s = s - (slopes[:, None, None, None] * dist)[None]
        p = jax.nn.softmax(s, axis=-1)
        pd = (p[:, :, 0] - lam * p[:, :, 1]).astype(v.dtype)
        return jnp.einsum("bhqs,bshe->bqhe", pd, v)

    o = lax.map(attend, (qb, tb)).swapaxes(0, 1).reshape(B, T, C_HEADS, 2 * C_HEAD_DIM)
    o = rmsnorm(o, sub_gain) * (1.0 - lam_init)
    return o.reshape(B, T, C_HEADS * 2 * C_HEAD_DIM) @ w_o


def mixer_fourier(h, w_o):
    B, T, D = h.shape
    hg = h.astype(F32).reshape(B, T, F_GROUPS, F_GROUP_DIM)
    f = jnp.fft.fft2(hg, axes=(1, 3), norm="ortho").real
    return f.reshape(B, T, D).astype(h.dtype) @ w_o


def swiglu(h, w_gate, w_up, w_down):
    return (jax.nn.silu(h @ w_gate) * (h @ w_up)) @ w_down


def encoder_trunk(x, attn_norm, ffn_norm, final_norm, a_w_qkv, a_q_gain, a_k_gain, a_w_o,
                  b_w_pool, b_scale, c_w_qkv, c_lambda, c_sub_gain, c_w_o, d_w_o,
                  w_gate, w_up, w_down):
    for i in range(DEPTH):
        kind, j = i % N_MIXERS, i // N_MIXERS
        h = rmsnorm(x, attn_norm[i])
        if kind == 0:
            y = mixer_gqa_axial(h, a_w_qkv[j], a_q_gain[j], a_k_gain[j], a_w_o[j])
        elif kind == 1:
            y = mixer_pool(h, b_w_pool[j], b_scale[j])
        elif kind == 2:
            lam_init = 0.8 - 0.6 * math.exp(-0.3 * i)
            y = mixer_diff_attn(h, c_w_qkv[j], c_lambda[j], c_sub_gain[j], c_w_o[j], lam_init)
        else:
            y = mixer_fourier(h, d_w_o[j])
        x = x + y
        x = x + swiglu(rmsnorm(x, ffn_norm[i]), w_gate[i], w_up[i], w_down[i])
    return rmsnorm(x, final_norm)


def setup_inputs(seed: int = 0) -> dict:
    key = jax.random.key(seed)
    ks = jax.random.split(key, 20)

    def nrm(k, shape, fan_in):
        return jax.random.normal(k, shape, F32) * (fan_in ** -0.5)

    def gain(k, shape):
        return 1.0 + 0.1 * jax.random.normal(k, shape, F32)

    a_qkv_w = (A_HEADS + 2 * A_KV_HEADS) * A_HEAD_DIM
    c_w = C_HEADS * 2 * C_HEAD_DIM
    return {
        "x_prompt": jax.random.normal(ks[0], (BATCH, SEQ, D_MODEL), F32),
        "x_sample": jax.random.normal(ks[1], (DEC_BATCH, DEC_SEQ, D_MODEL), F32),
        "attn_norm": gain(ks[2], (DEPTH, D_MODEL)),
        "ffn_norm": gain(ks[3], (DEPTH, D_MODEL)),
        "final_norm": gain(ks[4], (D_MODEL,)),
        "a_w_qkv": nrm(ks[5], (N_A, D_MODEL, a_qkv_w), D_MODEL),
        "a_q_gain": gain(ks[6], (N_A, A_HEAD_DIM)),
        "a_k_gain": gain(ks[7], (N_A, A_HEAD_DIM)),
        "a_w_o": nrm(ks[8], (N_A, A_HEADS * A_HEAD_DIM, D_MODEL), A_HEADS * A_HEAD_DIM),
        "b_w_pool": nrm(ks[9], (N_B, len(POOL_WINDOWS), POOL_GROUP, POOL_GROUP), POOL_GROUP),
        "b_scale": gain(ks[10], (N_B, D_MODEL)),
        "c_w_qkv": nrm(ks[11], (N_C, D_MODEL, 3 * c_w), D_MODEL),
        "c_lambda": 0.1 * jax.random.normal(ks[12], (N_C, 4, C_HEAD_DIM), F32),
        "c_sub_gain": gain(ks[13], (N_C, 2 * C_HEAD_DIM)),
        "c_w_o": nrm(ks[14], (N_C, c_w, D_MODEL), c_w),
        "d_w_o": nrm(ks[15], (N_D, D_MODEL, D_MODEL), D_MODEL),
        "w_gate": nrm(ks[16], (DEPTH, D_MODEL, D_FF), D_MODEL),
        "w_up": nrm(ks[17], (DEPTH, D_MODEL, D_FF), D_MODEL),
        "w_down": nrm(ks[18], (DEPTH, D_FF, D_MODEL), D_FF),
    }


def reference(x_prompt, x_sample, attn_norm, ffn_norm, final_norm, a_w_qkv, a_q_gain, a_k_gain,
              a_w_o, b_w_pool, b_scale, c_w_qkv, c_lambda, c_sub_gain, c_w_o, d_w_o,
              w_gate, w_up, w_down):
    y_prompt = encoder_trunk(x_prompt, attn_norm, ffn_norm, final_norm, a_w_qkv, a_q_gain,
                             a_k_gain, a_w_o, b_w_pool, b_scale, c_w_qkv, c_lambda, c_sub_gain,
                             c_w_o, d_w_o, w_gate, w_up, w_down)
    y_sample = encoder_trunk(x_sample, attn_norm, ffn_norm, final_norm, a_w_qkv, a_q_gain,
                             a_k_gain, a_w_o, b_w_pool, b_scale, c_w_qkv, c_lambda, c_sub_gain,
                             c_w_o, d_w_o, w_gate, w_up, w_down)
    return (y_prompt, y_sample)
```

```python
import functools
import math

import jax
import jax.numpy as jnp
from jax import lax
from jax.experimental import pallas as pl
from jax.experimental.pallas import tpu as pltpu

F32 = jnp.float32
BF16 = jnp.bfloat16

D_MODEL = 1024
DEPTH = 4
GRID_W = 64
NORM_EPS = 1e-6
A_HEADS = 16
A_KV_HEADS = 4
A_HEAD_DIM = 64
A_GROUP = A_HEADS // A_KV_HEADS
ROPE_THETA = 10000.0
POOL_WINDOWS = (2, 4, 8, 16)
POOL_GROUP = D_MODEL // len(POOL_WINDOWS)
POOL_HALO = 8
C_HEADS = 8
C_HEAD_DIM = 64
F_GROUPS = 4
F_GROUP_DIM = D_MODEL // F_GROUPS
D_FF = 2816
LOG2E = math.log2(math.e)

A_QK_W = (A_HEADS + A_KV_HEADS) * A_HEAD_DIM
A_QKV_W = A_QK_W + A_KV_HEADS * A_HEAD_DIM
LANES = 128
VMEM_LIMIT = 56 * 1024 * 1024

TM = 512
TQ = 256
FF_CHUNK = 256


def _params(*sem):
    return pltpu.CompilerParams(dimension_semantics=sem, vmem_limit_bytes=VMEM_LIMIT)


def _resident(shape):
    nd = len(shape)
    return pl.BlockSpec(shape, lambda *_: (0,) * nd, pipeline_mode=pl.Buffered(1))


def _rms(x, gain):
    ms = jnp.mean(x * x, axis=-1, keepdims=True)
    return x * lax.rsqrt(ms + NORM_EPS) * gain


def _split_bf16(x):
    hi = x.astype(BF16)
    lo = (x - hi.astype(F32)).astype(BF16)
    return hi, lo


def _dot(a, b):
    return jnp.dot(a, b, preferred_element_type=F32)


def _dot_nt(a, b):
    return lax.dot_general(a, b, (((1,), (1,)), ((), ())), preferred_element_type=F32)


def _ffn_kernel(final, x_ref, g_ref, wg_ref, wu_ref, wd_ref, fg_ref, o_ref):
    x = x_ref[...]
    h = _rms(x, g_ref[...]).astype(BF16)
    acc = x
    for c in range(D_FF // FF_CHUNK):
        g = _dot(h, wg_ref[c])
        u = _dot(h, wu_ref[c])
        a = (g * jax.nn.sigmoid(g) * u).astype(BF16)
        acc = acc + _dot(a, wd_ref[c])
    if final:
        acc = _rms(acc, fg_ref[...])
    o_ref[...] = acc


def _ffn(x, gain, wg, wu, wd, final_gain, final):
    n = x.shape[0]
    nc = D_FF // FF_CHUNK
    row = pl.BlockSpec((TM, D_MODEL), lambda i: (i, 0))
    return pl.pallas_call(
        functools.partial(_ffn_kernel, final),
        grid=(n // TM,),
        in_specs=[row, _resident((1, D_MODEL)), _resident((nc, D_MODEL, FF_CHUNK)),
                  _resident((nc, D_MODEL, FF_CHUNK)), _resident((nc, FF_CHUNK, D_MODEL)),
                  _resident((1, D_MODEL))],
        out_specs=row,
        out_shape=jax.ShapeDtypeStruct((n, D_MODEL), F32),
        compiler_params=_params("parallel"),
        name="ffn_final" if final else "ffn",
    )(x, gain, wg, wu, wd, final_gain)


def _qkv_a_kernel(x_ref, g_ref, w_ref, e_ref, et_ref, hg_ref, cos_ref, sin_ref,
                  q_ref, k_ref, v_ref):
    h = _rms(x_ref[...], g_ref[...]).astype(BF16)
    qkv = _dot(h, w_ref[...])
    qk = qkv[:, :A_QK_W]
    sq_hi, sq_lo = _split_bf16(qk * qk)
    ms = _dot(sq_hi, e_ref[...]) + _dot(sq_lo, e_ref[...])
    rs_hi, rs_lo = _split_bf16(lax.rsqrt(ms + NORM_EPS))
    rs = _dot(rs_hi, et_ref[...]) + _dot(rs_lo, et_ref[...])
    qn = qk * rs * hg_ref[...]
    lane = lax.broadcasted_iota(jnp.int32, qn.shape, 1)
    partner = jnp.where(lane % 2 == 0, pltpu.roll(qn, A_QK_W - 1, 1), pltpu.roll(qn, 1, 1))
    reps = A_QK_W // LANES
    cos = jnp.concatenate([cos_ref[...]] * reps, axis=1)
    sin = jnp.concatenate([sin_ref[...]] * reps, axis=1)
    out = qn * cos + partner * sin
    q_ref[...] = out[:, :A_HEADS * A_HEAD_DIM].astype(BF16)
    k_ref[...] = out[:, A_HEADS * A_HEAD_DIM:].astype(BF16)
    v_ref[...] = qkv[:, A_QK_W:].astype(BF16)


def _qkv_a(x, gain, w, e, et, head_gain, cos, sin, seq):
    n = x.shape[0]
    steps_per_seq = seq // TM
    row = lambda width: pl.BlockSpec((TM, width), lambda i: (i, 0))
    tab = pl.BlockSpec((TM, LANES), lambda i: (i % steps_per_seq, 0))
    kv_w = A_KV_HEADS * A_HEAD_DIM
    return pl.pallas_call(
        _qkv_a_kernel,
        grid=(n // TM,),
        in_specs=[row(D_MODEL), _resident((1, D_MODEL)), _resident((D_MODEL, A_QKV_W)),
                  _resident((A_QK_W, LANES)), _resident((LANES, A_QK_W)),
                  _resident((1, A_QK_W)), tab, tab],
        out_specs=[row(D_MODEL), row(kv_w), row(kv_w)],
        out_shape=[jax.ShapeDtypeStruct((n, D_MODEL), BF16),
                   jax.ShapeDtypeStruct((n, kv_w), BF16),
                   jax.ShapeDtypeStruct((n, kv_w), BF16)],
        compiler_params=_params("parallel"),
        name="qkv_a",
    )(x, gain, w, e, et, head_gain, cos, sin)


def _attn_a_kernel(x_ref, q_ref, k_ref, v_ref, wo_ref, o_ref, o_scr):
    hd = A_HEAD_DIM
    for g in range(A_KV_HEADS):
        kg = k_ref[0, :, g * hd:(g + 1) * hd]
        vg = v_ref[0, :, g * hd:(g + 1) * hd]
        for j in range(A_GROUP):
            h = g * A_GROUP + j
            s = _dot_nt(q_ref[0, :, h * hd:(h + 1) * hd], kg)
            m = jnp.max(s, axis=-1, keepdims=True)
            p = jnp.exp2(s - m)
            l = jnp.sum(p, axis=-1, keepdims=True)
            o_scr[:, h * hd:(h + 1) * hd] = _dot(p.astype(BF16), vg) / l
    o_ref[0] = x_ref[0] + _dot(o_scr[...].astype(BF16), wo_ref[...])


def _attn_a(x, q, k, v, wo):
    b, seq, _ = x.shape
    kv_w = A_KV_HEADS * A_HEAD_DIM
    tile = pl.BlockSpec((1, TQ, D_MODEL), lambda i, j: (i, j, 0))
    whole = pl.BlockSpec((1, seq, kv_w), lambda i, j: (i, 0, 0))
    return pl.pallas_call(
        _attn_a_kernel,
        grid=(b, seq // TQ),
        in_specs=[tile, tile, whole, whole, _resident((D_MODEL, D_MODEL))],
        out_specs=tile,
        out_shape=jax.ShapeDtypeStruct(x.shape, F32),
        scratch_shapes=[pltpu.VMEM((TQ, D_MODEL), F32)],
        compiler_params=_params("parallel", "parallel"),
        name="attn_a",
    )(x, q, k, v, wo)


def _pool_kernel(seq, x_ref, prev_ref, next_ref, g_ref, w_ref, sc_ref, o_ref, h_scr):
    i = pl.program_id(0)
    t0 = (i % (seq // TM)) * TM
    gain = g_ref[...]
    x = x_ref[...]
    h = _rms(x, gain)
    h_scr[0:POOL_HALO, :] = jnp.where(t0 > 0, _rms(prev_ref[...], gain), 0.0)
    h_scr[POOL_HALO:POOL_HALO + TM, :] = h
    h_scr[POOL_HALO + TM:, :] = jnp.where(t0 + TM < seq, _rms(next_ref[...], gain), 0.0)
    t = t0 + lax.broadcasted_iota(jnp.int32, (TM, 1), 0)
    ys = []
    for g, win in enumerate(POOL_WINDOWS):
        lanes = slice(g * POOL_GROUP, (g + 1) * POOL_GROUP)
        half = win // 2
        total = h_scr[POOL_HALO - half:POOL_HALO - half + TM, lanes]
        for d in range(1, win):
            total = total + h_scr[POOL_HALO - half + d:POOL_HALO - half + d + TM, lanes]
        lo = jnp.clip(t - half, 0, seq - 1)
        hi = jnp.clip(t + half - 1, 0, seq - 1)
        cnt = (hi - lo + 1).astype(F32)
        p = (total / cnt - h[:, lanes]).astype(BF16)
        ys.append(_dot(p, w_ref[g]))
    o_ref[...] = x + jnp.concatenate(ys, axis=1) * sc_ref[...]


def _pool(x, gain, w, scale, seq):
    n = x.shape[0]
    halo_per_tile = TM // POOL_HALO
    last_halo = n // POOL_HALO - 1
    row = pl.BlockSpec((TM, D_MODEL), lambda i: (i, 0))
    prev = pl.BlockSpec((POOL_HALO, D_MODEL),
                        lambda i: (jnp.maximum(i * halo_per_tile - 1, 0), 0))
    nxt = pl.BlockSpec((POOL_HALO, D_MODEL),
                       lambda i: (jnp.minimum((i + 1) * halo_per_tile, last_halo), 0))
    ng = len(POOL_WINDOWS)
    return pl.pallas_call(
        functools.partial(_pool_kernel, seq),
        grid=(n // TM,),
        in_specs=[row, prev, nxt, _resident((1, D_MODEL)),
                  _resident((ng, POOL_GROUP, POOL_GROUP)), _resident((1, D_MODEL))],
        out_specs=row,
        out_shape=jax.ShapeDtypeStruct((n, D_MODEL), F32),
        scratch_shapes=[pltpu.VMEM((TM + 2 * POOL_HALO, D_MODEL), F32)],
        compiler_params=_params("parallel"),
        name="pool",
    )(x, x, x, gain, w, scale)


def _qkv_c_kernel(x_ref, g_ref, w_ref, q_ref, k_ref, v_ref):
    h = _rms(x_ref[...], g_ref[...]).astype(BF16)
    width = 2 * C_HEADS * C_HEAD_DIM
    qkv = _dot(h, w_ref[...])
    q_ref[...] = (qkv[:, :width] * (C_HEAD_DIM ** -0.5 * LOG2E)).astype(BF16)
    k_ref[...] = qkv[:, width:2 * width].astype(BF16)
    v_ref[...] = qkv[:, 2 * width:].astype(BF16)


def _qkv_c(x, gain, w):
    n = x.shape[0]
    row = pl.BlockSpec((TM, D_MODEL), lambda i: (i, 0))
    out = jax.ShapeDtypeStruct((n, D_MODEL), BF16)
    return pl.pallas_call(
        _qkv_c_kernel,
        grid=(n // TM,),
        in_specs=[row, _resident((1, D_MODEL)), _resident((D_MODEL, 3 * D_MODEL))],
        out_specs=[row, row, row],
        out_shape=[out, out, out],
        compiler_params=_params("parallel"),
        name="qkv_c",
    )(x, gain, w)


def _attn_c_kernel(lam_init, x_ref, q_ref, k_ref, v_ref, lp_ref, sg_ref, wo_ref, o_ref, o_scr):
    seq = k_ref.shape[1]
    hd = C_HEAD_DIM
    lp = lp_ref[...]
    lam = (jnp.exp(jnp.sum(lp[0:1] * lp[1:2], axis=-1, keepdims=True))
           - jnp.exp(jnp.sum(lp[2:3] * lp[3:4], axis=-1, keepdims=True)) + lam_init)
    row = pl.program_id(1) * TQ + lax.broadcasted_iota(jnp.int32, (TQ, seq), 0)
    col = lax.broadcasted_iota(jnp.int32, (TQ, seq), 1)
    dist = jnp.abs(row - col).astype(F32)
    for h in range(C_HEADS):
        bias = dist * (2.0 ** (-(8.0 / C_HEADS) * (h + 1)) * LOG2E)
        es, scales = [], []
        for c in range(2):
            lanes = slice((2 * h + c) * hd, (2 * h + c + 1) * hd)
            s = _dot_nt(q_ref[0, :, lanes], k_ref[0, :, lanes]) - bias
            m = jnp.max(s, axis=-1, keepdims=True)
            e = jnp.exp2(s - m)
            es.append(e)
            scales.append(1.0 / jnp.sum(e, axis=-1, keepdims=True))
        pd = (es[0] * scales[0] - es[1] * (lam * scales[1])).astype(BF16)
        o = _dot(pd, v_ref[0, :, 2 * h * hd:2 * (h + 1) * hd])
        ms = jnp.mean(o * o, axis=-1, keepdims=True)
        o_scr[:, 2 * h * hd:2 * (h + 1) * hd] = (
            o * lax.rsqrt(ms + NORM_EPS) * sg_ref[...] * (1.0 - lam_init))
    o_ref[0] = x_ref[0] + _dot(o_scr[...].astype(BF16), wo_ref[...])


def _attn_c(x, q, k, v, lam_params, sub_gain, wo, lam_init):
    b, seq, _ = x.shape
    tile = pl.BlockSpec((1, TQ, D_MODEL), lambda i, j: (i, j, 0))
    whole = pl.BlockSpec((1, seq, D_MODEL), lambda i, j: (i, 0, 0))
    return pl.pallas_call(
        functools.partial(_attn_c_kernel, lam_init),
        grid=(b, seq // TQ),
        in_specs=[tile, tile, whole, whole, _resident((4, C_HEAD_DIM)),
                  _resident((1, 2 * C_HEAD_DIM)), _resident((D_MODEL, D_MODEL))],
        out_specs=tile,
        out_shape=jax.ShapeDtypeStruct(x.shape, F32),
        scratch_shapes=[pltpu.VMEM((TQ, D_MODEL), F32)],
        compiler_params=_params("parallel", "parallel"),
        name="attn_c",
    )(x, q, k, v, lam_params, sub_gain, wo)


def _hnorm_kernel(x_ref, g_ref, h_ref):
    h_ref[...] = _rms(x_ref[...], g_ref[...]).astype(BF16)


def _hnorm(x, gain):
    n = x.shape[0]
    row = pl.BlockSpec((TM, D_MODEL), lambda i: (i, 0))
    return pl.pallas_call(
        _hnorm_kernel,
        grid=(n // TM,),
        in_specs=[row, _resident((1, D_MODEL))],
        out_specs=row,
        out_shape=jax.ShapeDtypeStruct((n, D_MODEL), BF16),
        compiler_params=_params("parallel"),
        name="hnorm",
    )(x, gain)


def _fourier_kernel(x_ref, h_ref, ct_ref, st_ref, cc_ref, sc_ref, wo_ref, o_ref):
    h = h_ref[0]
    p = _dot(ct_ref[...], h).astype(BF16)
    q = _dot(st_ref[...], h).astype(BF16)
    fs = []
    for g in range(F_GROUPS):
        lanes = slice(g * F_GROUP_DIM, (g + 1) * F_GROUP_DIM)
        fs.append(_dot(p[:, lanes], cc_ref[...]) - _dot(q[:, lanes], sc_ref[...]))
    f = jnp.concatenate(fs, axis=1).astype(BF16)
    o_ref[0] = x_ref[0] + _dot(f, wo_ref[...])


def _fourier(x, h, ct, st, cc, sc, wo):
    b, seq, _ = x.shape
    tile = pl.BlockSpec((1, TM, D_MODEL), lambda i, j: (i, j, 0))
    whole = pl.BlockSpec((1, seq, D_MODEL), lambda i, j: (i, 0, 0))
    tab = pl.BlockSpec((TM, seq), lambda i, j: (j, 0))
    return pl.pallas_call(
        _fourier_kernel,
        grid=(b, seq // TM),
        in_specs=[tile, whole, tab, tab, _resident((F_GROUP_DIM, F_GROUP_DIM)),
                  _resident((F_GROUP_DIM, F_GROUP_DIM)), _resident((D_MODEL, D_MODEL))],
        out_specs=tile,
        out_shape=jax.ShapeDtypeStruct(x.shape, F32),
        compiler_params=_params("parallel", "parallel"),
        name="fourier",
    )(x, h, ct, st, cc, sc, wo)


def _rope_tables(seq):
    rows = seq // GRID_W
    r, c = jnp.meshgrid(jnp.arange(rows), jnp.arange(GRID_W), indexing="ij")
    r = r.reshape(-1).astype(F32)
    c = c.reshape(-1).astype(F32)
    n = A_HEAD_DIM // 4
    freqs = ROPE_THETA ** (-jnp.arange(n, dtype=F32) / n)
    ang = jnp.concatenate([r[:, None] * freqs, c[:, None] * freqs], axis=-1)
    cos = jnp.repeat(jnp.cos(ang), 2, axis=-1)
    sin = jnp.repeat(jnp.sin(ang), 2, axis=-1)
    sign = jnp.where(jnp.arange(A_HEAD_DIM) % 2 == 0, -1.0, 1.0).astype(F32)
    sin = sin * sign
    reps = LANES // A_HEAD_DIM
    return jnp.tile(cos, (1, reps)), jnp.tile(sin, (1, reps))


def _dft_tables(n, scale):
    k = jnp.arange(n, dtype=jnp.int32)
    kn = (k[:, None] * k[None, :]) % n
    ang = kn.astype(F32) * (2.0 * math.pi / n)
    return (jnp.cos(ang) * scale).astype(BF16), (jnp.sin(ang) * scale).astype(BF16)


def _head_reduce_tables():
    lane_head = jnp.arange(A_QK_W) // A_HEAD_DIM
    onehot = (lane_head[:, None] == jnp.arange(LANES)[None, :]).astype(F32)
    return (onehot / A_HEAD_DIM).astype(BF16), onehot.T.astype(BF16)


def _chunk_cols(w):
    return w.astype(BF16).reshape(D_MODEL, D_FF // FF_CHUNK, FF_CHUNK).transpose(1, 0, 2)


def _trunk(x, p):
    b, seq, _ = x.shape
    n = b * seq
    flat = lambda a: a.reshape(n, a.shape[-1])
    per_seq = lambda a: a.reshape(b, seq, a.shape[-1])
    x = flat(x)
    for i in range(DEPTH):
        kind = i % 4
        gain = p["attn_norm"][i][None]
        if kind == 0:
            q, k, v = _qkv_a(x, gain, p["a_w_qkv"], p["e"], p["et"], p["a_head_gain"],
                             p["cos"], p["sin"], seq)
            x = flat(_attn_a(per_seq(x), per_seq(q), per_seq(k), per_seq(v), p["a_w_o"]))
        elif kind == 1:
            x = _pool(x, gain, p["b_w_pool"], p["b_scale"], seq)
        elif kind == 2:
            lam_init = 0.8 - 0.6 * math.exp(-0.3 * i)
            q, k, v = _qkv_c(x, gain, p["c_w_qkv"])
            x = flat(_attn_c(per_seq(x), per_seq(q), per_seq(k), per_seq(v), p["c_lambda"],
                             p["c_sub_gain"], p["c_w_o"], lam_init))
        else:
            h = _hnorm(x, gain)
            x = flat(_fourier(per_seq(x), per_seq(h), p["ct"], p["st"], p["cc"], p["sc"],
                              p["d_w_o"]))
        x = _ffn(x, p["ffn_norm"][i][None], p["w_gate"][i], p["w_up"][i], p["w_down"][i],
                 p["final_norm"], final=(i == DEPTH - 1))
    return per_seq(x)


def kernel(x_prompt, x_sample, attn_norm, ffn_norm, final_norm, a_w_qkv, a_q_gain, a_k_gain,
           a_w_o, b_w_pool, b_scale, c_w_qkv, c_lambda, c_sub_gain, c_w_o, d_w_o,
           w_gate, w_up, w_down):
    seq = x_prompt.shape[1]
    assert x_sample.shape[1] == seq and seq % TM == 0 and seq % TQ == 0
    e, et = _head_reduce_tables()
    cos, sin = _rope_tables(seq)
    ct, st = _dft_tables(seq, seq ** -0.5)
    cc, sc = _dft_tables(F_GROUP_DIM, F_GROUP_DIM ** -0.5)
    head_gain = jnp.concatenate([jnp.tile(a_q_gain[0], A_HEADS) * (A_HEAD_DIM ** -0.5 * LOG2E),
                                 jnp.tile(a_k_gain[0], A_KV_HEADS)])[None]
    nc = D_FF // FF_CHUNK
    p = dict(
        attn_norm=attn_norm, ffn_norm=ffn_norm, final_norm=final_norm[None],
        a_w_qkv=a_w_qkv[0].astype(BF16), a_head_gain=head_gain, a_w_o=a_w_o[0].astype(BF16),
        e=e, et=et, cos=cos, sin=sin,
        b_w_pool=b_w_pool[0].astype(BF16), b_scale=b_scale[0][None],
        c_w_qkv=c_w_qkv[0].astype(BF16), c_lambda=c_lambda[0], c_sub_gain=c_sub_gain[0][None],
        c_w_o=c_w_o[0].astype(BF16),
        d_w_o=d_w_o[0].astype(BF16), ct=ct, st=st, cc=cc, sc=sc,
        w_gate=[_chunk_cols(w_gate[i]) for i in range(DEPTH)],
        w_up=[_chunk_cols(w_up[i]) for i in range(DEPTH)],
        w_down=[w_down[i].astype(BF16).reshape(nc, FF_CHUNK, D_MODEL) for i in range(DEPTH)],
    )
    return _trunk(x_prompt, p), _trunk(x_sample, p)
```

```python
import functools
import math

import jax
import jax.numpy as jnp
from jax import lax
from jax.experimental import pallas as pl
from jax.experimental.pallas import tpu as pltpu

F32 = jnp.float32
BF16 = jnp.bfloat16

D_MODEL = 1024
DEPTH = 4
LAYERS = tuple(range(DEPTH))
GRID_W = 64
NORM_EPS = 1e-6
A_HEADS = 16
A_KV_HEADS = 4
A_HEAD_DIM = 64
A_GROUP = A_HEADS // A_KV_HEADS
ROPE_THETA = 10000.0
POOL_WINDOWS = (2, 4, 8, 16)
POOL_GROUP = D_MODEL // len(POOL_WINDOWS)
POOL_HALO = 8
C_HEADS = 8
C_HEAD_DIM = 64
F_GROUPS = 4
F_GROUP_DIM = D_MODEL // F_GROUPS
D_FF = 2816
LOG2E = math.log2(math.e)

LANES = 128
BF16_ROWS = 16
VMEM_LIMIT = 56 * 1024 * 1024

A_Q_W = A_HEADS * A_HEAD_DIM
A_KDUP_W = A_KV_HEADS * LANES
A_QK_W = A_Q_W + A_KDUP_W
A_V_W = A_KV_HEADS * A_HEAD_DIM

TM = 512
TQ = 256
FF_CHUNK = 256


def _params(*sem):
    return pltpu.CompilerParams(dimension_semantics=sem, vmem_limit_bytes=VMEM_LIMIT)


def _resident(shape):
    nd = len(shape)
    return pl.BlockSpec(shape, lambda *_: (0,) * nd, pipeline_mode=pl.Buffered(1))


def _rms(x, gain):
    ms = jnp.mean(x * x, axis=-1, keepdims=True)
    return x * lax.rsqrt(ms + NORM_EPS) * gain


def _split_bf16(x):
    hi = x.astype(BF16)
    lo = (x - hi.astype(F32)).astype(BF16)
    return hi, lo


def _dot(a, b):
    return jnp.dot(a, b, preferred_element_type=F32)


def _dot_nt(a, b):
    return lax.dot_general(a, b, (((1,), (1,)), ((), ())), preferred_element_type=F32)


def _half_lanes_mask(rows, upper):
    lane = lax.broadcasted_iota(jnp.int32, (rows, LANES), 1)
    return lane >= LANES // 2 if upper else lane < LANES // 2


def _scores_t(k_ref, lanes, q, bias=None):
    half = k_ref.shape[1] // 2
    halves = []
    for r in range(2):
        s_t = _dot_nt(k_ref[0, r * half:(r + 1) * half, lanes], q)
        if bias is not None:
            s_t = s_t - bias[r * half:(r + 1) * half]
        halves.append(s_t)
    m = jnp.maximum(jnp.max(halves[0], axis=0, keepdims=True),
                    jnp.max(halves[1], axis=0, keepdims=True))
    return halves, m


def _probs_t(halves, m):
    return jnp.concatenate([jnp.exp2(s_t - m).astype(BF16) for s_t in halves], axis=0)


def _weighted_values_t(v_aug, p_t):
    d = v_aug.shape[0] - BF16_ROWS
    oa = _dot(v_aug, p_t)
    return oa[:d] / oa[d:d + 1]


def _pipelined(n, scores, probs, values):
    s = {0: scores(0)}
    if n > 1:
        s[1] = scores(1)
    p = {0: probs(*s.pop(0))}
    for i in range(n):
        if i + 2 < n:
            s[i + 2] = scores(i + 2)
        values(i, p.pop(i))
        if i + 1 < n:
            p[i + 1] = probs(*s.pop(i + 1))


def _ffn_kernel(final, x_ref, g_ref, wg_ref, wu_ref, wd_ref, fg_ref, o_ref):
    x = x_ref[...]
    h = _rms(x, g_ref[...]).astype(BF16)
    acc = x
    for c in range(D_FF // FF_CHUNK):
        g = _dot(h, wg_ref[c])
        u = _dot(h, wu_ref[c])
        a = (g * jax.nn.sigmoid(g) * u).astype(BF16)
        acc = acc + _dot(a, wd_ref[c])
    if final:
        acc = _rms(acc, fg_ref[...])
    o_ref[...] = acc


def _ffn(x, gain, wg, wu, wd, final_gain, final):
    n = x.shape[0]
    nc = D_FF // FF_CHUNK
    row = pl.BlockSpec((TM, D_MODEL), lambda i: (i, 0))
    return pl.pallas_call(
        functools.partial(_ffn_kernel, final),
        grid=(n // TM,),
        in_specs=[row, _resident((1, D_MODEL)), _resident((nc, D_MODEL, FF_CHUNK)),
                  _resident((nc, D_MODEL, FF_CHUNK)), _resident((nc, FF_CHUNK, D_MODEL)),
                  _resident((1, D_MODEL))],
        out_specs=row,
        out_shape=jax.ShapeDtypeStruct((n, D_MODEL), F32),
        compiler_params=_params("parallel"),
        name="ffn_final" if final else "ffn",
    )(x, gain, wg, wu, wd, final_gain)


def _qkv_a_kernel(x_ref, g_ref, w_ref, wvt_ref, e_ref, et_ref, hg_ref, cos_ref, sin_ref,
                  q_ref, k_ref, vt_ref):
    h = _rms(x_ref[...], g_ref[...]).astype(BF16)
    qk = _dot(h, w_ref[...])
    sq_hi, sq_lo = _split_bf16(qk * qk)
    ms = _dot(sq_hi, e_ref[...]) + _dot(sq_lo, e_ref[...])
    rs_hi, rs_lo = _split_bf16(lax.rsqrt(ms + NORM_EPS))
    rs = _dot(rs_hi, et_ref[...]) + _dot(rs_lo, et_ref[...])
    qn = qk * rs * hg_ref[...]
    lane = lax.broadcasted_iota(jnp.int32, qn.shape, 1)
    partner = jnp.where(lane % 2 == 0, pltpu.roll(qn, A_QK_W - 1, 1), pltpu.roll(qn, 1, 1))
    reps = A_QK_W // LANES
    cos = jnp.concatenate([cos_ref[...]] * reps, axis=1)
    sin = jnp.concatenate([sin_ref[...]] * reps, axis=1)
    out = qn * cos + partner * sin
    q_ref[...] = out[:, :A_Q_W].astype(BF16)
    k_ref[...] = out[:, A_Q_W:].astype(BF16)
    vt_ref[0] = _dot_nt(wvt_ref[...], h).astype(BF16)


def _qkv_a(x, gain, w, wvt, e, et, head_gain, cos, sin, seq):
    n = x.shape[0]
    spt = seq // TM
    row = lambda width: pl.BlockSpec((TM, width), lambda i: (i, 0))
    tab = pl.BlockSpec((TM, LANES), lambda i: (i % spt, 0))
    return pl.pallas_call(
        _qkv_a_kernel,
        grid=(n // TM,),
        in_specs=[row(D_MODEL), _resident((1, D_MODEL)), _resident((D_MODEL, A_QK_W)),
                  _resident((A_V_W, D_MODEL)), _resident((A_QK_W, LANES)),
                  _resident((LANES, A_QK_W)), _resident((1, A_QK_W)), tab, tab],
        out_specs=[row(A_Q_W), row(A_KDUP_W),
                   pl.BlockSpec((1, A_V_W, TM), lambda i: (i // spt, 0, i % spt))],
        out_shape=[jax.ShapeDtypeStruct((n, A_Q_W), BF16),
                   jax.ShapeDtypeStruct((n, A_KDUP_W), BF16),
                   jax.ShapeDtypeStruct((n // seq, A_V_W, seq), BF16)],
        compiler_params=_params("parallel"),
        name="qkv_a",
    )(x, gain, w, wvt, e, et, head_gain, cos, sin)


def _attn_a_kernel(x_ref, q_ref, k_ref, vt_ref, wo_ref, o_ref, ot_scr):
    seq = k_ref.shape[1]
    hd = A_HEAD_DIM
    ones = jnp.ones((BF16_ROWS, seq), BF16)

    def scores(h):
        g = h // A_GROUP
        q_pair = q_ref[0, :, (h // 2) * LANES:(h // 2 + 1) * LANES]
        q_h = jnp.where(_half_lanes_mask(TQ, h % 2 == 1), q_pair, jnp.zeros_like(q_pair))
        return _scores_t(k_ref, slice(g * LANES, (g + 1) * LANES), q_h)

    def values(h, p_t):
        g = h // A_GROUP
        v_aug = jnp.concatenate([vt_ref[0, g * hd:(g + 1) * hd, :], ones], axis=0)
        ot_scr[h * hd:(h + 1) * hd, :] = _weighted_values_t(v_aug, p_t)

    _pipelined(A_HEADS, scores, _probs_t, values)
    o = ot_scr[...].T.astype(BF16)
    o_ref[0] = x_ref[0] + _dot(o, wo_ref[...])


def _attn_a(x, q, k, vt, wo):
    b, seq, _ = x.shape
    tile = pl.BlockSpec((1, TQ, D_MODEL), lambda i, j: (i, j, 0))
    return pl.pallas_call(
        _attn_a_kernel,
        grid=(b, seq // TQ),
        in_specs=[tile, tile,
                  pl.BlockSpec((1, seq, A_KDUP_W), lambda i, j: (i, 0, 0)),
                  pl.BlockSpec((1, A_V_W, seq), lambda i, j: (i, 0, 0)),
                  _resident((D_MODEL, D_MODEL))],
        out_specs=tile,
        out_shape=jax.ShapeDtypeStruct(x.shape, F32),
        scratch_shapes=[pltpu.VMEM((D_MODEL, TQ), F32)],
        compiler_params=_params("parallel", "parallel"),
        name="attn_a",
    )(x, q, k, vt, wo)


def _pool_kernel(seq, x_ref, prev_ref, next_ref, g_ref, w_ref, sc_ref, o_ref, h_scr):
    i = pl.program_id(0)
    t0 = (i % (seq // TM)) * TM
    gain = g_ref[...]
    x = x_ref[...]
    h = _rms(x, gain)
    h_scr[0:POOL_HALO, :] = jnp.where(t0 > 0, _rms(prev_ref[...], gain), 0.0)
    h_scr[POOL_HALO:POOL_HALO + TM, :] = h
    h_scr[POOL_HALO + TM:, :] = jnp.where(t0 + TM < seq, _rms(next_ref[...], gain), 0.0)
    t = t0 + lax.broadcasted_iota(jnp.int32, (TM, 1), 0)
    ys = []
    for g, win in enumerate(POOL_WINDOWS):
        lanes = slice(g * POOL_GROUP, (g + 1) * POOL_GROUP)
        half = win // 2
        total = h_scr[POOL_HALO - half:POOL_HALO - half + TM, lanes]
        for d in range(1, win):
            total = total + h_scr[POOL_HALO - half + d:POOL_HALO - half + d + TM, lanes]
        lo = jnp.clip(t - half, 0, seq - 1)
        hi = jnp.clip(t + half - 1, 0, seq - 1)
        cnt = (hi - lo + 1).astype(F32)
        p = (total / cnt - h[:, lanes]).astype(BF16)
        ys.append(_dot(p, w_ref[g]))
    o_ref[...] = x + jnp.concatenate(ys, axis=1) * sc_ref[...]


def _pool(x, gain, w, scale, seq):
    n = x.shape[0]
    halo_per_tile = TM // POOL_HALO
    last_halo = n // POOL_HALO - 1
    row = pl.BlockSpec((TM, D_MODEL), lambda i: (i, 0))
    prev = pl.BlockSpec((POOL_HALO, D_MODEL),
                        lambda i: (jnp.maximum(i * halo_per_tile - 1, 0), 0))
    nxt = pl.BlockSpec((POOL_HALO, D_MODEL),
                       lambda i: (jnp.minimum((i + 1) * halo_per_tile, last_halo), 0))
    ng = len(POOL_WINDOWS)
    return pl.pallas_call(
        functools.partial(_pool_kernel, seq),
        grid=(n // TM,),
        in_specs=[row, prev, nxt, _resident((1, D_MODEL)),
                  _resident((ng, POOL_GROUP, POOL_GROUP)), _resident((1, D_MODEL))],
        out_specs=row,
        out_shape=jax.ShapeDtypeStruct((n, D_MODEL), F32),
        scratch_shapes=[pltpu.VMEM((TM + 2 * POOL_HALO, D_MODEL), F32)],
        compiler_params=_params("parallel"),
        name="pool",
    )(x, x, x, gain, w, scale)


def _qkv_c_kernel(x_ref, g_ref, w_ref, wvt_ref, q_ref, k_ref, vt_ref):
    h = _rms(x_ref[...], g_ref[...]).astype(BF16)
    width = 2 * C_HEADS * C_HEAD_DIM
    qk = _dot(h, w_ref[...])
    q_ref[...] = (qk[:, :width] * (C_HEAD_DIM ** -0.5 * LOG2E)).astype(BF16)
    k_ref[...] = qk[:, width:].astype(BF16)
    vt_ref[0] = _dot_nt(wvt_ref[...], h).astype(BF16)


def _qkv_c(x, gain, w, wvt, seq):
    n = x.shape[0]
    spt = seq // TM
    row = pl.BlockSpec((TM, D_MODEL), lambda i: (i, 0))
    out = jax.ShapeDtypeStruct((n, D_MODEL), BF16)
    return pl.pallas_call(
        _qkv_c_kernel,
        grid=(n // TM,),
        in_specs=[row, _resident((1, D_MODEL)), _resident((D_MODEL, 2 * D_MODEL)),
                  _resident((D_MODEL, D_MODEL))],
        out_specs=[row, row, pl.BlockSpec((1, D_MODEL, TM), lambda i: (i // spt, 0, i % spt))],
        out_shape=[out, out, jax.ShapeDtypeStruct((n // seq, D_MODEL, seq), BF16)],
        compiler_params=_params("parallel"),
        name="qkv_c",
    )(x, gain, w, wvt)


def _attn_c_kernel(lam_init, x_ref, q_ref, k_ref, vt_ref, lp_ref, sg_ref, wo_ref, o_ref, ot_scr):
    seq = k_ref.shape[1]
    hw = 2 * C_HEAD_DIM
    lp = lp_ref[...]
    lam = (jnp.exp(jnp.sum(lp[0:1] * lp[1:2], axis=-1, keepdims=True))
           - jnp.exp(jnp.sum(lp[2:3] * lp[3:4], axis=-1, keepdims=True)) + lam_init)
    key = lax.broadcasted_iota(jnp.int32, (seq, TQ), 0)
    qpos = pl.program_id(1) * TQ + lax.broadcasted_iota(jnp.int32, (seq, TQ), 1)
    dist = jnp.abs(key - qpos).astype(F32)
    ones = jnp.ones((BF16_ROWS, seq), BF16)
    os = [None, None]

    def scores(i):
        h, c = divmod(i, 2)
        bias = dist * (2.0 ** (-(8.0 / C_HEADS) * (h + 1)) * LOG2E)
        q_pair = q_ref[0, :, h * hw:(h + 1) * hw]
        q_c = jnp.where(_half_lanes_mask(TQ, c == 1), q_pair, jnp.zeros_like(q_pair))
        return _scores_t(k_ref, slice(h * hw, (h + 1) * hw), q_c, bias)

    def values(i, p_t):
        h, c = divmod(i, 2)
        v_aug = jnp.concatenate([vt_ref[0, h * hw:(h + 1) * hw, :], ones], axis=0)
        os[c] = _weighted_values_t(v_aug, p_t)
        if c == 1:
            o = os[0] - lam * os[1]
            ms = jnp.mean(o * o, axis=0, keepdims=True)
            ot_scr[h * hw:(h + 1) * hw, :] = (
                o * lax.rsqrt(ms + NORM_EPS) * sg_ref[...] * (1.0 - lam_init))

    _pipelined(2 * C_HEADS, scores, _probs_t, values)
    o_all = ot_scr[...].T.astype(BF16)
    o_ref[0] = x_ref[0] + _dot(o_all, wo_ref[...])


def _attn_c(x, q, k, vt, lam_params, sub_gain_t, wo, lam_init):
    b, seq, _ = x.shape
    tile = pl.BlockSpec((1, TQ, D_MODEL), lambda i, j: (i, j, 0))
    return pl.pallas_call(
        functools.partial(_attn_c_kernel, lam_init),
        grid=(b, seq // TQ),
        in_specs=[tile, tile,
                  pl.BlockSpec((1, seq, D_MODEL), lambda i, j: (i, 0, 0)),
                  pl.BlockSpec((1, D_MODEL, seq), lambda i, j: (i, 0, 0)),
                  _resident((4, C_HEAD_DIM)), _resident((2 * C_HEAD_DIM, TQ)),
                  _resident((D_MODEL, D_MODEL))],
        out_specs=tile,
        out_shape=jax.ShapeDtypeStruct(x.shape, F32),
        scratch_shapes=[pltpu.VMEM((D_MODEL, TQ), F32)],
        compiler_params=_params("parallel", "parallel"),
        name="attn_c",
    )(x, q, k, vt, lam_params, sub_gain_t, wo)


def _hnorm_kernel(x_ref, g_ref, h_ref):
    h_ref[...] = _rms(x_ref[...], g_ref[...]).astype(BF16)


def _hnorm(x, gain):
    n = x.shape[0]
    row = pl.BlockSpec((TM, D_MODEL), lambda i: (i, 0))
    return pl.pallas_call(
        _hnorm_kernel,
        grid=(n // TM,),
        in_specs=[row, _resident((1, D_MODEL))],
        out_specs=row,
        out_shape=jax.ShapeDtypeStruct((n, D_MODEL), BF16),
        compiler_params=_params("parallel"),
        name="hnorm",
    )(x, gain)


def _fourier_kernel(x_ref, h_ref, ct_ref, st_ref, cc_ref, sc_ref, wo_ref, o_ref):
    h = h_ref[0]
    p = _dot(ct_ref[...], h).astype(BF16)
    q = _dot(st_ref[...], h).astype(BF16)
    fs = []
    for g in range(F_GROUPS):
        lanes = slice(g * F_GROUP_DIM, (g + 1) * F_GROUP_DIM)
        fs.append(_dot(p[:, lanes], cc_ref[...]) - _dot(q[:, lanes], sc_ref[...]))
    f = jnp.concatenate(fs, axis=1).astype(BF16)
    o_ref[0] = x_ref[0] + _dot(f, wo_ref[...])


def _fourier(x, h, ct, st, cc, sc, wo):
    b, seq, _ = x.shape
    tile = pl.BlockSpec((1, TM, D_MODEL), lambda i, j: (i, j, 0))
    whole = pl.BlockSpec((1, seq, D_MODEL), lambda i, j: (i, 0, 0))
    tab = pl.BlockSpec((TM, seq), lambda i, j: (j, 0))
    return pl.pallas_call(
        _fourier_kernel,
        grid=(b, seq // TM),
        in_specs=[tile, whole, tab, tab, _resident((F_GROUP_DIM, F_GROUP_DIM)),
                  _resident((F_GROUP_DIM, F_GROUP_DIM)), _resident((D_MODEL, D_MODEL))],
        out_specs=tile,
        out_shape=jax.ShapeDtypeStruct(x.shape, F32),
        compiler_params=_params("parallel", "parallel"),
        name="fourier",
    )(x, h, ct, st, cc, sc, wo)


def _rope_tables(seq):
    rows = seq // GRID_W
    r, c = jnp.meshgrid(jnp.arange(rows), jnp.arange(GRID_W), indexing="ij")
    r = r.reshape(-1).astype(F32)
    c = c.reshape(-1).astype(F32)
    n = A_HEAD_DIM // 4
    freqs = ROPE_THETA ** (-jnp.arange(n, dtype=F32) / n)
    ang = jnp.concatenate([r[:, None] * freqs, c[:, None] * freqs], axis=-1)
    cos = jnp.repeat(jnp.cos(ang), 2, axis=-1)
    sin = jnp.repeat(jnp.sin(ang), 2, axis=-1)
    sign = jnp.where(jnp.arange(A_HEAD_DIM) % 2 == 0, -1.0, 1.0).astype(F32)
    sin = sin * sign
    reps = LANES // A_HEAD_DIM
    return jnp.tile(cos, (1, reps)), jnp.tile(sin, (1, reps))


def _dft_tables(n, scale):
    k = jnp.arange(n, dtype=jnp.int32)
    kn = (k[:, None] * k[None, :]) % n
    ang = kn.astype(F32) * (2.0 * math.pi / n)
    return (jnp.cos(ang) * scale).astype(BF16), (jnp.sin(ang) * scale).astype(BF16)


def _head_reduce_tables():
    lane_head = jnp.arange(A_QK_W) // A_HEAD_DIM
    onehot = (lane_head[:, None] == jnp.arange(LANES)[None, :]).astype(F32)
    return (onehot / A_HEAD_DIM).astype(BF16), onehot.T.astype(BF16)


def _chunk_cols(w):
    return w.astype(BF16).reshape(D_MODEL, D_FF // FF_CHUNK, FF_CHUNK).transpose(1, 0, 2)


def _trunk(x, p):
    b, seq, _ = x.shape
    n = b * seq
    flat = lambda a: a.reshape(n, a.shape[-1])
    per_seq = lambda a: a.reshape(b, seq, a.shape[-1])
    x = flat(x)
    for i in LAYERS:
        kind = i % 4
        gain = p["attn_norm"][i][None]
        if kind == 0:
            q, k, vt = _qkv_a(x, gain, p["a_w_qk"], p["a_w_vt"], p["e"], p["et"],
                              p["a_head_gain"], p["cos"], p["sin"], seq)
            x = flat(_attn_a(per_seq(x), per_seq(q), per_seq(k), vt, p["a_w_o"]))
        elif kind == 1:
            x = _pool(x, gain, p["b_w_pool"], p["b_scale"], seq)
        elif kind == 2:
            lam_init = 0.8 - 0.6 * math.exp(-0.3 * i)
            q, k, vt = _qkv_c(x, gain, p["c_w_qk"], p["c_w_vt"], seq)
            x = flat(_attn_c(per_seq(x), per_seq(q), per_seq(k), vt, p["c_lambda"],
                             p["c_sub_gain_t"], p["c_w_o"], lam_init))
        else:
            h = _hnorm(x, gain)
            x = flat(_fourier(per_seq(x), per_seq(h), p["ct"], p["st"], p["cc"], p["sc"],
                              p["d_w_o"]))
        x = _ffn(x, p["ffn_norm"][i][None], p["w_gate"][i], p["w_up"][i], p["w_down"][i],
                 p["final_norm"], final=(i == DEPTH - 1))
    return per_seq(x)


def kernel(x_prompt, x_sample, attn_norm, ffn_norm, final_norm, a_w_qkv, a_q_gain, a_k_gain,
           a_w_o, b_w_pool, b_scale, c_w_qkv, c_lambda, c_sub_gain, c_w_o, d_w_o,
           w_gate, w_up, w_down):
    seq = x_prompt.shape[1]
    assert x_sample.shape[1] == seq and seq % TM == 0 and seq % TQ == 0
    e, et = _head_reduce_tables()
    cos, sin = _rope_tables(seq)
    ct, st = _dft_tables(seq, seq ** -0.5)
    cc, sc = _dft_tables(F_GROUP_DIM, F_GROUP_DIM ** -0.5)
    wa = a_w_qkv[0]
    wk = wa[:, A_Q_W:A_Q_W + A_V_W].reshape(D_MODEL, A_KV_HEADS, A_HEAD_DIM)
    a_w_qk = jnp.concatenate(
        [wa[:, :A_Q_W], jnp.concatenate([wk, wk], axis=-1).reshape(D_MODEL, A_KDUP_W)], axis=1)
    head_gain = jnp.concatenate([jnp.tile(a_q_gain[0], A_HEADS) * (A_HEAD_DIM ** -0.5 * LOG2E),
                                 jnp.tile(a_k_gain[0], 2 * A_KV_HEADS)])[None]
    wc = c_w_qkv[0]
    nc = D_FF // FF_CHUNK
    p = dict(
        attn_norm=attn_norm, ffn_norm=ffn_norm, final_norm=final_norm[None],
        a_w_qk=a_w_qk.astype(BF16), a_w_vt=wa[:, A_Q_W + A_V_W:].T.astype(BF16),
        a_head_gain=head_gain, a_w_o=a_w_o[0].astype(BF16),
        e=e, et=et, cos=cos, sin=sin,
        b_w_pool=b_w_pool[0].astype(BF16), b_scale=b_scale[0][None],
        c_w_qk=wc[:, :2 * D_MODEL].astype(BF16), c_w_vt=wc[:, 2 * D_MODEL:].T.astype(BF16),
        c_lambda=c_lambda[0],
        c_sub_gain_t=jnp.broadcast_to(c_sub_gain[0][:, None], (2 * C_HEAD_DIM, TQ)),
        c_w_o=c_w_o[0].astype(BF16),
        d_w_o=d_w_o[0].astype(BF16), ct=ct, st=st, cc=cc, sc=sc,
        w_gate=[_chunk_cols(w_gate[i]) for i in range(DEPTH)],
        w_up=[_chunk_cols(w_up[i]) for i in range(DEPTH)],
        w_down=[w_down[i].astype(BF16).reshape(nc, FF_CHUNK, D_MODEL) for i in range(DEPTH)],
    )
    return _trunk(x_prompt, p), _trunk(x_sample, p)
```

```python
import functools
import math

import jax
import jax.numpy as jnp
import ml_dtypes
from jax import lax
from jax.experimental import pallas as pl
from jax.experimental.pallas import tpu as pltpu

F32 = jnp.float32
BF16 = jnp.bfloat16

D_MODEL = 1024
DEPTH = 4
LAYERS = tuple(range(DEPTH))
GRID_W = 64
NORM_EPS = 1e-6
A_HEADS = 16
A_KV_HEADS = 4
A_HEAD_DIM = 64
A_GROUP = A_HEADS // A_KV_HEADS
ROPE_THETA = 10000.0
POOL_WINDOWS = (2, 4, 8, 16)
POOL_GROUP = D_MODEL // len(POOL_WINDOWS)
POOL_HALO = 8
C_HEADS = 8
C_HEAD_DIM = 64
F_GROUPS = 4
F_GROUP_DIM = D_MODEL // F_GROUPS
D_FF = 2816
LOG2E = math.log2(math.e)


def _bf16_parts(x, n):
    parts = []
    for _ in range(n):
        parts.append(float(ml_dtypes.bfloat16(x)))
        x -= parts[-1]
    return tuple(parts)


assert C_HEADS == 8
ALIBI_E = 2.0 ** (-8.0 / C_HEADS) * LOG2E
ALIBI_E_PARTS = _bf16_parts(ALIBI_E, 3)

LANES = 128
BF16_ROWS = 16
VMEM_LIMIT = 56 * 1024 * 1024

A_Q_W = A_HEADS * A_HEAD_DIM
A_KDUP_W = A_KV_HEADS * LANES
A_QK_W = A_Q_W + A_KDUP_W
A_V_W = A_KV_HEADS * A_HEAD_DIM

TM = 512
TQ = 256
FF_CHUNK = 256


def _params(*sem):
    return pltpu.CompilerParams(dimension_semantics=sem, vmem_limit_bytes=VMEM_LIMIT)


def _resident(shape):
    nd = len(shape)
    return pl.BlockSpec(shape, lambda *_: (0,) * nd, pipeline_mode=pl.Buffered(1))


def _rms(x, gain):
    ms = jnp.mean(x * x, axis=-1, keepdims=True)
    return x * lax.rsqrt(ms + NORM_EPS) * gain


def _split_bf16(x):
    hi = x.astype(BF16)
    lo = (x - hi.astype(F32)).astype(BF16)
    return hi, lo


def _dot(a, b):
    return jnp.dot(a, b, preferred_element_type=F32)


def _dot_nt(a, b):
    return lax.dot_general(a, b, (((1,), (1,)), ((), ())), preferred_element_type=F32)


def _half_lanes_mask(rows, upper):
    lane = lax.broadcasted_iota(jnp.int32, (rows, LANES), 1)
    return lane >= LANES // 2 if upper else lane < LANES // 2


def _scores_t(k_halves, q, head_bias=None):
    halves = [_dot_nt(k, q) for k in k_halves]
    if head_bias is not None:
        n = head_bias.shape[0]
        halves[0] = jnp.concatenate([halves[0][:n] - head_bias, halves[0][n:]], axis=0)
    m = jnp.maximum(jnp.max(halves[0], axis=0, keepdims=True),
                    jnp.max(halves[1], axis=0, keepdims=True))
    return halves, m


def _probs_t(halves, m):
    return jnp.concatenate([jnp.exp2(s_t - m).astype(BF16) for s_t in halves], axis=0)


def _weighted_values_t(v_aug, p_t):
    d = v_aug.shape[0] - BF16_ROWS
    oa = _dot(v_aug, p_t)
    return oa[:d] / oa[d:d + 1]


def _pipelined(n, scores, probs, values):
    s = {0: scores(0)}
    if n > 1:
        s[1] = scores(1)
    p = {0: probs(*s.pop(0))}
    for i in range(n):
        if i + 2 < n:
            s[i + 2] = scores(i + 2)
        values(i, p.pop(i))
        if i + 1 < n:
            p[i + 1] = probs(*s.pop(i + 1))


def _ffn_kernel(final, x_ref, g_ref, wg_ref, wu_ref, wd_ref, fg_ref, o_ref):
    x = x_ref[...]
    h = _rms(x, g_ref[...]).astype(BF16)
    acc = x
    for c in range(D_FF // FF_CHUNK):
        g = _dot(h, wg_ref[c])
        u = _dot(h, wu_ref[c])
        a = (g * jax.nn.sigmoid(g) * u).astype(BF16)
        acc = acc + _dot(a, wd_ref[c])
    if final:
        acc = _rms(acc, fg_ref[...])
    o_ref[...] = acc


def _ffn(x, gain, wg, wu, wd, final_gain, final):
    n = x.shape[0]
    nc = D_FF // FF_CHUNK
    row = pl.BlockSpec((TM, D_MODEL), lambda i: (i, 0))
    return pl.pallas_call(
        functools.partial(_ffn_kernel, final),
        grid=(n // TM,),
        in_specs=[row, _resident((1, D_MODEL)), _resident((nc, D_MODEL, FF_CHUNK)),
                  _resident((nc, D_MODEL, FF_CHUNK)), _resident((nc, FF_CHUNK, D_MODEL)),
                  _resident((1, D_MODEL))],
        out_specs=row,
        out_shape=jax.ShapeDtypeStruct((n, D_MODEL), F32),
        compiler_params=_params("parallel"),
        name="ffn_final" if final else "ffn",
    )(x, gain, wg, wu, wd, final_gain)


def _qkv_a_kernel(x_ref, g_ref, w_ref, wvt_ref, e_ref, et_ref, hg_ref, cos_ref, sin_ref,
                  q_ref, k_ref, vt_ref):
    h = _rms(x_ref[...], g_ref[...]).astype(BF16)
    qk = _dot(h, w_ref[...])
    sq_hi, sq_lo = _split_bf16(qk * qk)
    ms = _dot(sq_hi, e_ref[...]) + _dot(sq_lo, e_ref[...])
    rs_hi, rs_lo = _split_bf16(lax.rsqrt(ms + NORM_EPS))
    rs = _dot(rs_hi, et_ref[...]) + _dot(rs_lo, et_ref[...])
    qn = qk * rs * hg_ref[...]
    lane = lax.broadcasted_iota(jnp.int32, qn.shape, 1)
    partner = jnp.where(lane % 2 == 0, pltpu.roll(qn, A_QK_W - 1, 1), pltpu.roll(qn, 1, 1))
    reps = A_QK_W // LANES
    cos = jnp.concatenate([cos_ref[...]] * reps, axis=1)
    sin = jnp.concatenate([sin_ref[...]] * reps, axis=1)
    out = qn * cos + partner * sin
    q_ref[...] = out[:, :A_Q_W].astype(BF16)
    k_ref[...] = out[:, A_Q_W:].astype(BF16)
    vt_ref[0] = _dot_nt(wvt_ref[...], h).astype(BF16)


def _qkv_a(x, gain, w, wvt, e, et, head_gain, cos, sin, seq):
    n = x.shape[0]
    spt = seq // TM
    row = lambda width: pl.BlockSpec((TM, width), lambda i: (i, 0))
    tab = pl.BlockSpec((TM, LANES), lambda i: (i % spt, 0))
    return pl.pallas_call(
        _qkv_a_kernel,
        grid=(n // TM,),
        in_specs=[row(D_MODEL), _resident((1, D_MODEL)), _resident((D_MODEL, A_QK_W)),
                  _resident((A_V_W, D_MODEL)), _resident((A_QK_W, LANES)),
                  _resident((LANES, A_QK_W)), _resident((1, A_QK_W)), tab, tab],
        out_specs=[row(A_Q_W), row(A_KDUP_W),
                   pl.BlockSpec((1, A_V_W, TM), lambda i: (i // spt, 0, i % spt))],
        out_shape=[jax.ShapeDtypeStruct((n, A_Q_W), BF16),
                   jax.ShapeDtypeStruct((n, A_KDUP_W), BF16),
                   jax.ShapeDtypeStruct((n // seq, A_V_W, seq), BF16)],
        compiler_params=_params("parallel"),
        name="qkv_a",
    )(x, gain, w, wvt, e, et, head_gain, cos, sin)


def _attn_a_kernel(x_ref, q_ref, k_ref, vt_ref, wo_ref, o_ref, ot_scr):
    seq = k_ref.shape[1]
    hd = A_HEAD_DIM
    ones = jnp.ones((BF16_ROWS, seq), BF16)

    def scores(h):
        g = h // A_GROUP
        q_pair = q_ref[0, :, (h // 2) * LANES:(h // 2 + 1) * LANES]
        q_h = jnp.where(_half_lanes_mask(TQ, h % 2 == 1), q_pair, jnp.zeros_like(q_pair))
        k_halves = [k_ref[0, r * (seq // 2):(r + 1) * (seq // 2), g * LANES:(g + 1) * LANES]
                    for r in range(2)]
        return _scores_t(k_halves, q_h)

    def values(h, p_t):
        g = h // A_GROUP
        v_aug = jnp.concatenate([vt_ref[0, g * hd:(g + 1) * hd, :], ones], axis=0)
        ot_scr[h * hd:(h + 1) * hd, :] = _weighted_values_t(v_aug, p_t)

    _pipelined(A_HEADS, scores, _probs_t, values)
    o = ot_scr[...].T.astype(BF16)
    o_ref[0] = x_ref[0] + _dot(o, wo_ref[...])


def _attn_a(x, q, k, vt, wo):
    b, seq, _ = x.shape
    tile = pl.BlockSpec((1, TQ, D_MODEL), lambda i, j: (i, j, 0))
    return pl.pallas_call(
        _attn_a_kernel,
        grid=(b, seq // TQ),
        in_specs=[tile, tile,
                  pl.BlockSpec((1, seq, A_KDUP_W), lambda i, j: (i, 0, 0)),
                  pl.BlockSpec((1, A_V_W, seq), lambda i, j: (i, 0, 0)),
                  _resident((D_MODEL, D_MODEL))],
        out_specs=tile,
        out_shape=jax.ShapeDtypeStruct(x.shape, F32),
        scratch_shapes=[pltpu.VMEM((D_MODEL, TQ), F32)],
        compiler_params=_params("parallel", "parallel"),
        name="attn_a",
    )(x, q, k, vt, wo)


def _pool_kernel(seq, x_ref, prev_ref, next_ref, g_ref, w_ref, sc_ref, o_ref, h_scr):
    i = pl.program_id(0)
    t0 = (i % (seq // TM)) * TM
    gain = g_ref[...]
    x = x_ref[...]
    h = _rms(x, gain)
    h_scr[0:POOL_HALO, :] = jnp.where(t0 > 0, _rms(prev_ref[...], gain), 0.0)
    h_scr[POOL_HALO:POOL_HALO + TM, :] = h
    h_scr[POOL_HALO + TM:, :] = jnp.where(t0 + TM < seq, _rms(next_ref[...], gain), 0.0)
    t = t0 + lax.broadcasted_iota(jnp.int32, (TM, 1), 0)
    ys = []
    for g, win in enumerate(POOL_WINDOWS):
        lanes = slice(g * POOL_GROUP, (g + 1) * POOL_GROUP)
        half = win // 2
        total = h_scr[POOL_HALO - half:POOL_HALO - half + TM, lanes]
        for d in range(1, win):
            total = total + h_scr[POOL_HALO - half + d:POOL_HALO - half + d + TM, lanes]
        lo = jnp.clip(t - half, 0, seq - 1)
        hi = jnp.clip(t + half - 1, 0, seq - 1)
        cnt = (hi - lo + 1).astype(F32)
        p = (total / cnt - h[:, lanes]).astype(BF16)
        ys.append(_dot(p, w_ref[g]))
    o_ref[...] = x + jnp.concatenate(ys, axis=1) * sc_ref[...]


def _pool(x, gain, w, scale, seq):
    n = x.shape[0]
    halo_per_tile = TM // POOL_HALO
    last_halo = n // POOL_HALO - 1
    row = pl.BlockSpec((TM, D_MODEL), lambda i: (i, 0))
    prev = pl.BlockSpec((POOL_HALO, D_MODEL),
                        lambda i: (jnp.maximum(i * halo_per_tile - 1, 0), 0))
    nxt = pl.BlockSpec((POOL_HALO, D_MODEL),
                       lambda i: (jnp.minimum((i + 1) * halo_per_tile, last_halo), 0))
    ng = len(POOL_WINDOWS)
    return pl.pallas_call(
        functools.partial(_pool_kernel, seq),
        grid=(n // TM,),
        in_specs=[row, prev, nxt, _resident((1, D_MODEL)),
                  _resident((ng, POOL_GROUP, POOL_GROUP)), _resident((1, D_MODEL))],
        out_specs=row,
        out_shape=jax.ShapeDtypeStruct((n, D_MODEL), F32),
        scratch_shapes=[pltpu.VMEM((TM + 2 * POOL_HALO, D_MODEL), F32)],
        compiler_params=_params("parallel"),
        name="pool",
    )(x, x, x, gain, w, scale)


def _qkv_c_kernel(x_ref, g_ref, w_ref, wvt_ref, q_ref, k_ref, vt_ref):
    h = _rms(x_ref[...], g_ref[...]).astype(BF16)
    width = 2 * C_HEADS * C_HEAD_DIM
    qk = _dot(h, w_ref[...])
    q_ref[...] = (qk[:, :width] * (C_HEAD_DIM ** -0.5 * LOG2E)).astype(BF16)
    k_ref[...] = qk[:, width:].astype(BF16)
    vt_ref[0] = _dot_nt(wvt_ref[...], h).astype(BF16)


def _qkv_c(x, gain, w, wvt, seq):
    n = x.shape[0]
    spt = seq // TM
    row = pl.BlockSpec((TM, D_MODEL), lambda i: (i, 0))
    out = jax.ShapeDtypeStruct((n, D_MODEL), BF16)
    return pl.pallas_call(
        _qkv_c_kernel,
        grid=(n // TM,),
        in_specs=[row, _resident((1, D_MODEL)), _resident((D_MODEL, 2 * D_MODEL)),
                  _resident((D_MODEL, D_MODEL))],
        out_specs=[row, row, pl.BlockSpec((1, D_MODEL, TM), lambda i: (i // spt, 0, i % spt))],
        out_shape=[out, out, jax.ShapeDtypeStruct((n // seq, D_MODEL, seq), BF16)],
        compiler_params=_params("parallel"),
        name="qkv_c",
    )(x, gain, w, wvt)


def _lane_pattern(lane, values):
    out = jnp.zeros(lane.shape, F32)
    for l, v in enumerate(values):
        out = jnp.where(lane == l, v, out)
    return out


def _attn_c_kernel(lam_init, nb, x_ref, q_ref, *refs):
    k_refs, vt_refs = refs[:nb], refs[nb:2 * nb]
    lp_ref, sg_ref, wo_ref, o_ref, ot_scr = refs[2 * nb:]
    hw = 2 * C_HEAD_DIM
    it = pl.program_id(1)
    lp = lp_ref[...]
    lam = (jnp.exp(jnp.sum(lp[0:1] * lp[1:2], axis=-1, keepdims=True))
           - jnp.exp(jnp.sum(lp[2:3] * lp[3:4], axis=-1, keepdims=True)) + lam_init)

    lane = lax.broadcasted_iota(jnp.int32, (TQ, LANES), 1)
    offset = lax.broadcasted_iota(jnp.int32, (TQ, LANES), 0).astype(F32)
    e_parts = list(ALIBI_E_PARTS)
    k_aug = [jnp.zeros((TQ, LANES), BF16)]
    for d in range(1, nb):
        sigma = jnp.where(it + d < nb, -1.0, 1.0)
        j0 = (((it + d) % nb) * TQ).astype(F32)
        k_aug.append((sigma * _lane_pattern(lane, 2 * e_parts + 3 * [j0] + 3 * [offset]))
                     .astype(BF16))
    i0 = (it * TQ).astype(F32)
    q_aug0 = _lane_pattern(lane, 3 * [-offset] + 3 * [-i0] + 2 * e_parts)
    q_aug = [(q_aug0 * 2.0 ** -h).astype(BF16) for h in range(C_HEADS)]
    own_dist = jnp.abs(lax.broadcasted_iota(jnp.int32, (TQ, TQ), 0)
                       - lax.broadcasted_iota(jnp.int32, (TQ, TQ), 1)).astype(F32)
    ones = jnp.ones((BF16_ROWS, nb * TQ), BF16)
    os = [None, None]

    def scores(i):
        h, c = divmod(i, 2)
        lanes = slice(h * hw, (h + 1) * hw)
        q_pair = q_ref[0, :, lanes]
        q_c = jnp.where(_half_lanes_mask(TQ, c == 1), q_pair, jnp.zeros_like(q_pair))
        q_full = jnp.concatenate([q_c, q_aug[h]], axis=1)
        k_halves = [
            jnp.concatenate(
                [jnp.concatenate([k_refs[d][0, :, lanes], k_aug[d]], axis=1)
                 for d in range(r * nb // 2, (r + 1) * nb // 2)], axis=0)
            for r in range(2)]
        return _scores_t(k_halves, q_full, own_dist * (ALIBI_E * 2.0 ** -h))

    def values(i, p_t):
        h, c = divmod(i, 2)
        v_t = jnp.concatenate([vt_refs[d][0, h * hw:(h + 1) * hw, :] for d in range(nb)], axis=1)
        os[c] = _weighted_values_t(jnp.concatenate([v_t, ones], axis=0), p_t)
        if c == 1:
            o = os[0] - lam * os[1]
            ms = jnp.mean(o * o, axis=0, keepdims=True)
            ot_scr[h * hw:(h + 1) * hw, :] = (
                o * lax.rsqrt(ms + NORM_EPS) * sg_ref[...] * (1.0 - lam_init))

    _pipelined(2 * C_HEADS, scores, _probs_t, values)
    o_all = ot_scr[...].T.astype(BF16)
    o_ref[0] = x_ref[0] + _dot(o_all, wo_ref[...])


def _attn_c(x, q, k, vt, lam_params, sub_gain_t, wo, lam_init):
    b, seq, _ = x.shape
    nb = seq // TQ
    tile = pl.BlockSpec((1, TQ, D_MODEL), lambda i, j: (i, j, 0))
    k_blocks = [pl.BlockSpec((1, TQ, D_MODEL), lambda i, j, d=d: (i, (j + d) % nb, 0))
                for d in range(nb)]
    vt_blocks = [pl.BlockSpec((1, D_MODEL, TQ), lambda i, j, d=d: (i, 0, (j + d) % nb))
                 for d in range(nb)]
    return pl.pallas_call(
        functools.partial(_attn_c_kernel, lam_init, nb),
        grid=(b, nb),
        in_specs=[tile, tile, *k_blocks, *vt_blocks,
                  _resident((4, C_HEAD_DIM)), _resident((2 * C_HEAD_DIM, TQ)),
                  _resident((D_MODEL, D_MODEL))],
        out_specs=tile,
        out_shape=jax.ShapeDtypeStruct(x.shape, F32),
        scratch_shapes=[pltpu.VMEM((D_MODEL, TQ), F32)],
        compiler_params=_params("parallel", "parallel"),
        name="attn_c",
    )(x, q, *([k] * nb), *([vt] * nb), lam_params, sub_gain_t, wo)


def _hnorm_kernel(x_ref, g_ref, h_ref):
    h_ref[...] = _rms(x_ref[...], g_ref[...]).astype(BF16)


def _hnorm(x, gain):
    n = x.shape[0]
    row = pl.BlockSpec((TM, D_MODEL), lambda i: (i, 0))
    return pl.pallas_call(
        _hnorm_kernel,
        grid=(n // TM,),
        in_specs=[row, _resident((1, D_MODEL))],
        out_specs=row,
        out_shape=jax.ShapeDtypeStruct((n, D_MODEL), BF16),
        compiler_params=_params("parallel"),
        name="hnorm",
    )(x, gain)


def _fourier_kernel(x_ref, h_ref, ct_ref, st_ref, cc_ref, sc_ref, wo_ref, o_ref):
    h = h_ref[0]
    p = _dot(ct_ref[...], h).astype(BF16)
    q = _dot(st_ref[...], h).astype(BF16)
    fs = []
    for g in range(F_GROUPS):
        lanes = slice(g * F_GROUP_DIM, (g + 1) * F_GROUP_DIM)
        fs.append(_dot(p[:, lanes], cc_ref[...]) - _dot(q[:, lanes], sc_ref[...]))
    f = jnp.concatenate(fs, axis=1).astype(BF16)
    o_ref[0] = x_ref[0] + _dot(f, wo_ref[...])


def _fourier(x, h, ct, st, cc, sc, wo):
    b, seq, _ = x.shape
    tile = pl.BlockSpec((1, TM, D_MODEL), lambda i, j: (i, j, 0))
    whole = pl.BlockSpec((1, seq, D_MODEL), lambda i, j: (i, 0, 0))
    tab = pl.BlockSpec((TM, seq), lambda i, j: (j, 0))
    return pl.pallas_call(
        _fourier_kernel,
        grid=(b, seq // TM),
        in_specs=[tile, whole, tab, tab, _resident((F_GROUP_DIM, F_GROUP_DIM)),
                  _resident((F_GROUP_DIM, F_GROUP_DIM)), _resident((D_MODEL, D_MODEL))],
        out_specs=tile,
        out_shape=jax.ShapeDtypeStruct(x.shape, F32),
        compiler_params=_params("parallel", "parallel"),
        name="fourier",
    )(x, h, ct, st, cc, sc, wo)


def _rope_tables(seq):
    rows = seq // GRID_W
    r, c = jnp.meshgrid(jnp.arange(rows), jnp.arange(GRID_W), indexing="ij")
    r = r.reshape(-1).astype(F32)
    c = c.reshape(-1).astype(F32)
    n = A_HEAD_DIM // 4
    freqs = ROPE_THETA ** (-jnp.arange(n, dtype=F32) / n)
    ang = jnp.concatenate([r[:, None] * freqs, c[:, None] * freqs], axis=-1)
    cos = jnp.repeat(jnp.cos(ang), 2, axis=-1)
    sin = jnp.repeat(jnp.sin(ang), 2, axis=-1)
    sign = jnp.where(jnp.arange(A_HEAD_DIM) % 2 == 0, -1.0, 1.0).astype(F32)
    sin = sin * sign
    reps = LANES // A_HEAD_DIM
    return jnp.tile(cos, (1, reps)), jnp.tile(sin, (1, reps))


def _dft_tables(n, scale):
    k = jnp.arange(n, dtype=jnp.int32)
    kn = (k[:, None] * k[None, :]) % n
    ang = kn.astype(F32) * (2.0 * math.pi / n)
    return (jnp.cos(ang) * scale).astype(BF16), (jnp.sin(ang) * scale).astype(BF16)


def _head_reduce_tables():
    lane_head = jnp.arange(A_QK_W) // A_HEAD_DIM
    onehot = (lane_head[:, None] == jnp.arange(LANES)[None, :]).astype(F32)
    return (onehot / A_HEAD_DIM).astype(BF16), onehot.T.astype(BF16)


def _chunk_cols(w):
    return w.astype(BF16).reshape(D_MODEL, D_FF // FF_CHUNK, FF_CHUNK).transpose(1, 0, 2)


def _trunk(x, p):
    b, seq, _ = x.shape
    n = b * seq
    flat = lambda a: a.reshape(n, a.shape[-1])
    per_seq = lambda a: a.reshape(b, seq, a.shape[-1])
    x = flat(x)
    for i in LAYERS:
        kind = i % 4
        gain = p["attn_norm"][i][None]
        if kind == 0:
            q, k, vt = _qkv_a(x, gain, p["a_w_qk"], p["a_w_vt"], p["e"], p["et"],
                              p["a_head_gain"], p["cos"], p["sin"], seq)
            x = flat(_attn_a(per_seq(x), per_seq(q), per_seq(k), vt, p["a_w_o"]))
        elif kind == 1:
            x = _pool(x, gain, p["b_w_pool"], p["b_scale"], seq)
        elif kind == 2:
            lam_init = 0.8 - 0.6 * math.exp(-0.3 * i)
            q, k, vt = _qkv_c(x, gain, p["c_w_qk"], p["c_w_vt"], seq)
            x = flat(_attn_c(per_seq(x), per_seq(q), per_seq(k), vt, p["c_lambda"],
                             p["c_sub_gain_t"], p["c_w_o"], lam_init))
        else:
            h = _hnorm(x, gain)
            x = flat(_fourier(per_seq(x), per_seq(h), p["ct"], p["st"], p["cc"], p["sc"],
                              p["d_w_o"]))
        x = _ffn(x, p["ffn_norm"][i][None], p["w_gate"][i], p["w_up"][i], p["w_down"][i],
                 p["final_norm"], final=(i == DEPTH - 1))
    return per_seq(x)


def kernel(x_prompt, x_sample, attn_norm, ffn_norm, final_norm, a_w_qkv, a_q_gain, a_k_gain,
           a_w_o, b_w_pool, b_scale, c_w_qkv, c_lambda, c_sub_gain, c_w_o, d_w_o,
           w_gate, w_up, w_down):
    seq = x_prompt.shape[1]
    assert x_sample.shape[1] == seq and seq % TM == 0 and seq % TQ == 0
    e, et = _head_reduce_tables()
    cos, sin = _rope_tables(seq)
    ct, st = _dft_tables(seq, seq ** -0.5)
    cc, sc = _dft_tables(F_GROUP_DIM, F_GROUP_DIM ** -0.5)
    wa = a_w_qkv[0]
    wk = wa[:, A_Q_W:A_Q_W + A_V_W].reshape(D_MODEL, A_KV_HEADS, A_HEAD_DIM)
    a_w_qk = jnp.concatenate(
        [wa[:, :A_Q_W], jnp.concatenate([wk, wk], axis=-1).reshape(D_MODEL, A_KDUP_W)], axis=1)
    head_gain = jnp.concatenate([jnp.tile(a_q_gain[0], A_HEADS) * (A_HEAD_DIM ** -0.5 * LOG2E),
                                 jnp.tile(a_k_gain[0], 2 * A_KV_HEADS)])[None]
    wc = c_w_qkv[0]
    nc = D_FF // FF_CHUNK
    p = dict(
        attn_norm=attn_norm, ffn_norm=ffn_norm, final_norm=final_norm[None],
        a_w_qk=a_w_qk.astype(BF16), a_w_vt=wa[:, A_Q_W + A_V_W:].T.astype(BF16),
        a_head_gain=head_gain, a_w_o=a_w_o[0].astype(BF16),
        e=e, et=et, cos=cos, sin=sin,
        b_w_pool=b_w_pool[0].astype(BF16), b_scale=b_scale[0][None],
        c_w_qk=wc[:, :2 * D_MODEL].astype(BF16), c_w_vt=wc[:, 2 * D_MODEL:].T.astype(BF16),
        c_lambda=c_lambda[0],
        c_sub_gain_t=jnp.broadcast_to(c_sub_gain[0][:, None], (2 * C_HEAD_DIM, TQ)),
        c_w_o=c_w_o[0].astype(BF16),
        d_w_o=d_w_o[0].astype(BF16), ct=ct, st=st, cc=cc, sc=sc,
        w_gate=[_chunk_cols(w_gate[i]) for i in range(DEPTH)],
        w_up=[_chunk_cols(w_up[i]) for i in range(DEPTH)],
        w_down=[w_down[i].astype(BF16).reshape(nc, FF_CHUNK, D_MODEL) for i in range(DEPTH)],
    )
    return _trunk(x_prompt, p), _trunk(x_sample, p)
```

```python
import functools
import math

import jax
import jax.numpy as jnp
import ml_dtypes
from jax import lax
from jax.experimental import pallas as pl
from jax.experimental.pallas import tpu as pltpu

F32 = jnp.float32
BF16 = jnp.bfloat16

D_MODEL = 1024
DEPTH = 4
LAYERS = tuple(range(DEPTH))
GRID_W = 64
NORM_EPS = 1e-6
A_HEADS = 16
A_KV_HEADS = 4
A_HEAD_DIM = 64
A_GROUP = A_HEADS // A_KV_HEADS
ROPE_THETA = 10000.0
POOL_WINDOWS = (2, 4, 8, 16)
POOL_GROUP = D_MODEL // len(POOL_WINDOWS)
POOL_HALO = 8
C_HEADS = 8
C_HEAD_DIM = 64
F_GROUPS = 4
F_GROUP_DIM = D_MODEL // F_GROUPS
D_FF = 2816
LOG2E = math.log2(math.e)


def _bf16_parts(x, n):
    parts = []
    for _ in range(n):
        parts.append(float(ml_dtypes.bfloat16(x)))
        x -= parts[-1]
    return tuple(parts)


assert C_HEADS == 8
ALIBI_E = 2.0 ** (-8.0 / C_HEADS) * LOG2E
ALIBI_E_PARTS = _bf16_parts(ALIBI_E, 3)

LANES = 128
BF16_ROWS = 16
VMEM_LIMIT = 56 * 1024 * 1024

A_Q_W = A_HEADS * A_HEAD_DIM
A_KDUP_W = A_KV_HEADS * LANES
A_QK_W = A_Q_W + A_KDUP_W
A_V_W = A_KV_HEADS * A_HEAD_DIM

TM = 512
TQ = 256
FF_CHUNK = 256
KEY_SPLITS = 2


def _params(*sem):
    return pltpu.CompilerParams(dimension_semantics=sem, vmem_limit_bytes=VMEM_LIMIT)


def _resident(shape):
    nd = len(shape)
    return pl.BlockSpec(shape, lambda *_: (0,) * nd, pipeline_mode=pl.Buffered(1))


def _rms(x, gain):
    ms = jnp.mean(x * x, axis=-1, keepdims=True)
    return x * lax.rsqrt(ms + NORM_EPS) * gain


def _split_bf16(x):
    hi = x.astype(BF16)
    lo = (x - hi.astype(F32)).astype(BF16)
    return hi, lo


def _dot(a, b):
    return jnp.dot(a, b, preferred_element_type=F32)


def _dot_nt(a, b):
    return lax.dot_general(a, b, (((1,), (1,)), ((), ())), preferred_element_type=F32)


def _half_lanes_mask(rows, upper):
    lane = lax.broadcasted_iota(jnp.int32, (rows, LANES), 1)
    return lane >= LANES // 2 if upper else lane < LANES // 2


def _scores_t(k_halves, q, head_bias=None):
    halves = [_dot_nt(k, q) for k in k_halves]
    if head_bias is not None:
        n = head_bias.shape[0]
        halves[0] = jnp.concatenate([halves[0][:n] - head_bias, halves[0][n:]], axis=0)
    m = functools.reduce(jnp.maximum, [jnp.max(s_t, axis=0, keepdims=True) for s_t in halves])
    return halves, m


def _probs_t(halves, m):
    return jnp.concatenate([jnp.exp2(s_t - m).astype(BF16) for s_t in halves], axis=0)


def _weighted_values_t(v_aug, p_t):
    d = v_aug.shape[0] - BF16_ROWS
    oa = _dot(v_aug, p_t)
    return oa[:d] / oa[d:d + 1]


def _pipelined(n, scores, probs, values, filler=None):
    s = {0: scores(0)}
    if n > 1:
        s[1] = scores(1)
    p = {0: probs(*s.pop(0))}
    for i in range(n):
        if i + 2 < n:
            s[i + 2] = scores(i + 2)
        if i + 1 < n:
            p[i + 1] = probs(*s.pop(i + 1))
        values(i, p.pop(i))
        if filler is not None:
            filler(i)


class _FfnInPieces:
    def __init__(self, x, g_ref, wg_ref, wu_ref, wd_ref):
        self.h = _rms(x, g_ref[...]).astype(BF16)
        self.acc = x
        self.refs = (wg_ref, wu_ref, wd_ref)
        self.n = wg_ref.shape[0]
        self.issued = 0
        self.act = None

    def _advance(self):
        wg_ref, wu_ref, wd_ref = self.refs
        c = self.issued
        nxt = None
        if c < self.n:
            g = _dot(self.h, wg_ref[c])
            u = _dot(self.h, wu_ref[c])
            nxt = (g * jax.nn.sigmoid(g) * u).astype(BF16)
        if self.act is not None:
            self.acc = self.acc + _dot(self.act, wd_ref[c - 1])
        self.act = nxt
        self.issued = min(c + 1, self.n + 1)

    def spread(self, slots):
        total = self.n + 1

        def filler(i):
            for _ in range(total * i // slots, total * (i + 1) // slots):
                self._advance()
        return filler

    def result(self):
        while self.issued <= self.n:
            self._advance()
        return self.acc


def _ffn_apply(x, g_ref, wg_ref, wu_ref, wd_ref):
    h = _rms(x, g_ref[...]).astype(BF16)
    acc = x
    for c in range(D_FF // FF_CHUNK):
        g = _dot(h, wg_ref[c])
        u = _dot(h, wu_ref[c])
        a = (g * jax.nn.sigmoid(g) * u).astype(BF16)
        acc = acc + _dot(a, wd_ref[c])
    return acc


def _ffn_specs():
    nc = D_FF // FF_CHUNK
    return [_resident((1, D_MODEL)), _resident((nc, D_MODEL, FF_CHUNK)),
            _resident((nc, D_MODEL, FF_CHUNK)), _resident((nc, FF_CHUNK, D_MODEL))]


def _ffn_kernel(mode, x_ref, g_ref, wg_ref, wu_ref, wd_ref, ng_ref, o_ref, *h_ref):
    acc = _ffn_apply(x_ref[...], g_ref, wg_ref, wu_ref, wd_ref)
    normed = _rms(acc, ng_ref[...])
    o_ref[...] = normed if mode == "final" else acc
    if mode == "emit":
        h_ref[0][...] = normed.astype(BF16)


def _ffn(x, gain, wg, wu, wd, next_gain, mode, in_tile=lambda i: i):
    n = x.shape[0]
    row = pl.BlockSpec((TM, D_MODEL), lambda i: (i, 0))
    out = jax.ShapeDtypeStruct((n, D_MODEL), F32)
    emit = mode == "emit"
    return pl.pallas_call(
        functools.partial(_ffn_kernel, mode),
        grid=(n // TM,),
        in_specs=[pl.BlockSpec((TM, D_MODEL), lambda i: (in_tile(i), 0)), *_ffn_specs(),
                  _resident((1, D_MODEL))],
        out_specs=[row, row] if emit else row,
        out_shape=[out, jax.ShapeDtypeStruct((n, D_MODEL), BF16)] if emit else out,
        compiler_params=_params("parallel"),
        name="ffn_" + mode,
    )(x, gain, wg, wu, wd, next_gain)


def _qkv_a_kernel(x_ref, g_ref, w_ref, wvt_ref, e_ref, et_ref, hg_ref, cos_ref, sin_ref,
                  q_ref, k_ref, vt_ref):
    h = _rms(x_ref[...], g_ref[...]).astype(BF16)
    qk = _dot(h, w_ref[...])
    sq_hi, sq_lo = _split_bf16(qk * qk)
    ms = _dot(sq_hi, e_ref[...]) + _dot(sq_lo, e_ref[...])
    rs_hi, rs_lo = _split_bf16(lax.rsqrt(ms + NORM_EPS))
    rs = _dot(rs_hi, et_ref[...]) + _dot(rs_lo, et_ref[...])
    qn = qk * rs * hg_ref[...]
    lane = lax.broadcasted_iota(jnp.int32, qn.shape, 1)
    partner = jnp.where(lane % 2 == 0, pltpu.roll(qn, A_QK_W - 1, 1), pltpu.roll(qn, 1, 1))
    reps = A_QK_W // LANES
    cos = jnp.concatenate([cos_ref[...]] * reps, axis=1)
    sin = jnp.concatenate([sin_ref[...]] * reps, axis=1)
    out = qn * cos + partner * sin
    q_ref[...] = out[:, :A_Q_W].astype(BF16)
    k_ref[...] = out[:, A_Q_W:].astype(BF16)
    vt_ref[0] = _dot_nt(wvt_ref[...], h).astype(BF16)


def _qkv_a(x, gain, w, wvt, e, et, head_gain, cos, sin, seq):
    n = x.shape[0]
    spt = seq // TM
    row = lambda width: pl.BlockSpec((TM, width), lambda i: (i, 0))
    tab = pl.BlockSpec((TM, LANES), lambda i: (i % spt, 0))
    return pl.pallas_call(
        _qkv_a_kernel,
        grid=(n // TM,),
        in_specs=[row(D_MODEL), _resident((1, D_MODEL)), _resident((D_MODEL, A_QK_W)),
                  _resident((A_V_W, D_MODEL)), _resident((A_QK_W, LANES)),
                  _resident((LANES, A_QK_W)), _resident((1, A_QK_W)), tab, tab],
        out_specs=[row(A_Q_W), row(A_KDUP_W),
                   pl.BlockSpec((1, A_V_W, TM), lambda i: (i // spt, 0, i % spt))],
        out_shape=[jax.ShapeDtypeStruct((n, A_Q_W), BF16),
                   jax.ShapeDtypeStruct((n, A_KDUP_W), BF16),
                   jax.ShapeDtypeStruct((n // seq, A_V_W, seq), BF16)],
        compiler_params=_params("parallel"),
        name="qkv_a",
    )(x, gain, w, wvt, e, et, head_gain, cos, sin)


def _lagged_ffn(x1_scr, g_ref, wg_ref, wu_ref, wd_ref):
    s = pl.program_id(0)

    @pl.when(s == 0)
    def _():
        x1_scr[1] = jnp.zeros(x1_scr.shape[1:], F32)

    return _FfnInPieces(x1_scr[(s + 1) % 2], g_ref, wg_ref, wu_ref, wd_ref)


def _attn_a_kernel(x_ref, q_ref, k_ref, vt_ref, wo_ref, g_ref, wg_ref, wu_ref, wd_ref,
                   o_ref, ot_scr, x1_scr):
    seq = k_ref.shape[1]
    hd = A_HEAD_DIM
    ones = jnp.ones((BF16_ROWS, seq), BF16)
    ffn = _lagged_ffn(x1_scr, g_ref, wg_ref, wu_ref, wd_ref)

    def scores(h):
        g = h // A_GROUP
        q_pair = q_ref[0, :, (h // 2) * LANES:(h // 2 + 1) * LANES]
        q_h = jnp.where(_half_lanes_mask(TQ, h % 2 == 1), q_pair, jnp.zeros_like(q_pair))
        part = seq // KEY_SPLITS
        k_halves = [k_ref[0, r * part:(r + 1) * part, g * LANES:(g + 1) * LANES]
                    for r in range(KEY_SPLITS)]
        return _scores_t(k_halves, q_h)

    def values(h, p_t):
        g = h // A_GROUP
        v_aug = jnp.concatenate([vt_ref[0, g * hd:(g + 1) * hd, :], ones], axis=0)
        ot_scr[h * hd:(h + 1) * hd, :] = _weighted_values_t(v_aug, p_t)

    _pipelined(A_HEADS, scores, _probs_t, values, ffn.spread(A_HEADS))
    o = ot_scr[...].T.astype(BF16)
    x1_scr[pl.program_id(0) % 2] = x_ref[0] + _dot(o, wo_ref[...])
    o_ref[0] = ffn.result()


def _lagged_tiles(b, nt):
    last = b * nt - 1
    cur = lambda s: jnp.minimum(s, last)
    prev = lambda s: jnp.maximum(s - 1, 0)
    return cur, prev


def _attn_a(x, q, k, vt, wo, ffn_gain, wg, wu, wd):
    b, seq, _ = x.shape
    nt = seq // TQ
    cur, prev = _lagged_tiles(b, nt)
    tile = pl.BlockSpec((1, TQ, D_MODEL), lambda s: (cur(s) // nt, cur(s) % nt, 0))
    return pl.pallas_call(
        _attn_a_kernel,
        grid=(b * nt + 1,),
        in_specs=[tile, tile,
                  pl.BlockSpec((1, seq, A_KDUP_W), lambda s: (cur(s) // nt, 0, 0)),
                  pl.BlockSpec((1, A_V_W, seq), lambda s: (cur(s) // nt, 0, 0)),
                  _resident((D_MODEL, D_MODEL)), *_ffn_specs()],
        out_specs=pl.BlockSpec((1, TQ, D_MODEL), lambda s: (prev(s) // nt, prev(s) % nt, 0)),
        out_shape=jax.ShapeDtypeStruct(x.shape, F32),
        scratch_shapes=[pltpu.VMEM((D_MODEL, TQ), F32), pltpu.VMEM((2, TQ, D_MODEL), F32)],
        compiler_params=_params("arbitrary"),
        name="attn_a_ffn",
    )(x, q, k, vt, wo, ffn_gain, wg, wu, wd)


def _pool_kernel(seq, x_ref, prev_ref, next_ref, g_ref, w_ref, sc_ref, fg_ref, wg_ref, wu_ref,
                 wd_ref, o_ref, h_scr):
    i = pl.program_id(0)
    t0 = (i % (seq // TM)) * TM
    gain = g_ref[...]
    x = x_ref[...]
    h = _rms(x, gain)
    h_scr[0:POOL_HALO, :] = jnp.where(t0 > 0, _rms(prev_ref[...], gain), 0.0)
    h_scr[POOL_HALO:POOL_HALO + TM, :] = h
    h_scr[POOL_HALO + TM:, :] = jnp.where(t0 + TM < seq, _rms(next_ref[...], gain), 0.0)
    t = t0 + lax.broadcasted_iota(jnp.int32, (TM, 1), 0)
    ys = []
    for g, win in enumerate(POOL_WINDOWS):
        lanes = slice(g * POOL_GROUP, (g + 1) * POOL_GROUP)
        half = win // 2
        total = h_scr[POOL_HALO - half:POOL_HALO - half + TM, lanes]
        for d in range(1, win):
            total = total + h_scr[POOL_HALO - half + d:POOL_HALO - half + d + TM, lanes]
        lo = jnp.clip(t - half, 0, seq - 1)
        hi = jnp.clip(t + half - 1, 0, seq - 1)
        cnt = (hi - lo + 1).astype(F32)
        p = (total / cnt - h[:, lanes]).astype(BF16)
        ys.append(_dot(p, w_ref[g]))
    x1 = x + jnp.concatenate(ys, axis=1) * sc_ref[...]
    o_ref[...] = _ffn_apply(x1, fg_ref, wg_ref, wu_ref, wd_ref)


def _pool(x, gain, w, scale, seq, ffn_gain, wg, wu, wd):
    n = x.shape[0]
    halo_per_tile = TM // POOL_HALO
    last_halo = n // POOL_HALO - 1
    row = pl.BlockSpec((TM, D_MODEL), lambda i: (i, 0))
    prev = pl.BlockSpec((POOL_HALO, D_MODEL),
                        lambda i: (jnp.maximum(i * halo_per_tile - 1, 0), 0))
    nxt = pl.BlockSpec((POOL_HALO, D_MODEL),
                       lambda i: (jnp.minimum((i + 1) * halo_per_tile, last_halo), 0))
    ng = len(POOL_WINDOWS)
    return pl.pallas_call(
        functools.partial(_pool_kernel, seq),
        grid=(n // TM,),
        in_specs=[row, prev, nxt, _resident((1, D_MODEL)),
                  _resident((ng, POOL_GROUP, POOL_GROUP)), _resident((1, D_MODEL)),
                  *_ffn_specs()],
        out_specs=row,
        out_shape=jax.ShapeDtypeStruct((n, D_MODEL), F32),
        scratch_shapes=[pltpu.VMEM((TM + 2 * POOL_HALO, D_MODEL), F32)],
        compiler_params=_params("parallel"),
        name="pool_ffn",
    )(x, x, x, gain, w, scale, ffn_gain, wg, wu, wd)


def _qkv_c_kernel(x_ref, g_ref, w_ref, wvt_ref, q_ref, k_ref, vt_ref):
    h = _rms(x_ref[...], g_ref[...]).astype(BF16)
    width = 2 * C_HEADS * C_HEAD_DIM
    qk = _dot(h, w_ref[...])
    q_ref[...] = (qk[:, :width] * (C_HEAD_DIM ** -0.5 * LOG2E)).astype(BF16)
    k_ref[...] = qk[:, width:].astype(BF16)
    vt_ref[0] = _dot_nt(wvt_ref[...], h).astype(BF16)


def _qkv_c(x, gain, w, wvt, seq):
    n = x.shape[0]
    spt = seq // TM
    row = pl.BlockSpec((TM, D_MODEL), lambda i: (i, 0))
    out = jax.ShapeDtypeStruct((n, D_MODEL), BF16)
    return pl.pallas_call(
        _qkv_c_kernel,
        grid=(n // TM,),
        in_specs=[row, _resident((1, D_MODEL)), _resident((D_MODEL, 2 * D_MODEL)),
                  _resident((D_MODEL, D_MODEL))],
        out_specs=[row, row, pl.BlockSpec((1, D_MODEL, TM), lambda i: (i // spt, 0, i % spt))],
        out_shape=[out, out, jax.ShapeDtypeStruct((n // seq, D_MODEL, seq), BF16)],
        compiler_params=_params("parallel"),
        name="qkv_c",
    )(x, gain, w, wvt)


def _lane_pattern(lane, values):
    out = jnp.zeros(lane.shape, F32)
    for l, v in enumerate(values):
        out = jnp.where(lane == l, v, out)
    return out


def _attn_c_kernel(lam_init, nb, x_ref, q_ref, *refs):
    k_refs, vt_refs = refs[:nb], refs[nb:2 * nb]
    lp_ref, sg_ref, wo_ref, o_ref, ot_scr = refs[2 * nb:]
    hw = 2 * C_HEAD_DIM
    it = pl.program_id(1)
    lp = lp_ref[...]
    lam = (jnp.exp(jnp.sum(lp[0:1] * lp[1:2], axis=-1, keepdims=True))
           - jnp.exp(jnp.sum(lp[2:3] * lp[3:4], axis=-1, keepdims=True)) + lam_init)

    lane = lax.broadcasted_iota(jnp.int32, (TQ, LANES), 1)
    offset = lax.broadcasted_iota(jnp.int32, (TQ, LANES), 0).astype(F32)
    e_parts = list(ALIBI_E_PARTS)
    k_aug = [jnp.zeros((TQ, LANES), BF16)]
    for d in range(1, nb):
        sigma = jnp.where(it + d < nb, -1.0, 1.0)
        j0 = (((it + d) % nb) * TQ).astype(F32)
        k_aug.append((sigma * _lane_pattern(lane, 2 * e_parts + 3 * [j0] + 3 * [offset]))
                     .astype(BF16))
    i0 = (it * TQ).astype(F32)
    q_aug0 = _lane_pattern(lane, 3 * [-offset] + 3 * [-i0] + 2 * e_parts)
    q_aug = [(q_aug0 * 2.0 ** -h).astype(BF16) for h in range(C_HEADS)]
    own_dist = jnp.abs(lax.broadcasted_iota(jnp.int32, (TQ, TQ), 0)
                       - lax.broadcasted_iota(jnp.int32, (TQ, TQ), 1)).astype(F32)
    ones = jnp.ones((BF16_ROWS, nb * TQ), BF16)
    os = [None, None]

    def scores(i):
        h, c = divmod(i, 2)
        lanes = slice(h * hw, (h + 1) * hw)
        q_pair = q_ref[0, :, lanes]
        q_c = jnp.where(_half_lanes_mask(TQ, c == 1), q_pair, jnp.zeros_like(q_pair))
        q_full = jnp.concatenate([q_c, q_aug[h]], axis=1)
        k_halves = [
            jnp.concatenate(
                [jnp.concatenate([k_refs[d][0, :, lanes], k_aug[d]], axis=1)
                 for d in range(r * nb // KEY_SPLITS, (r + 1) * nb // KEY_SPLITS)], axis=0)
            for r in range(KEY_SPLITS)]
        return _scores_t(k_halves, q_full, own_dist * (ALIBI_E * 2.0 ** -h))

    def values(i, p_t):
        h, c = divmod(i, 2)
        v_t = jnp.concatenate([vt_refs[d][0, h * hw:(h + 1) * hw, :] for d in range(nb)], axis=1)
        os[c] = _weighted_values_t(jnp.concatenate([v_t, ones], axis=0), p_t)
        if c == 1:
            o = os[0] - lam * os[1]
            ms = jnp.mean(o * o, axis=0, keepdims=True)
            ot_scr[h * hw:(h + 1) * hw, :] = (
                o * lax.rsqrt(ms + NORM_EPS) * sg_ref[...] * (1.0 - lam_init))

    _pipelined(2 * C_HEADS, scores, _probs_t, values)
    o_all = ot_scr[...].T.astype(BF16)
    o_ref[0] = x_ref[0] + _dot(o_all, wo_ref[...])


def _attn_c(x, q, k, vt, lam_params, sub_gain_t, wo, lam_init):
    b, seq, _ = x.shape
    nb = seq // TQ
    tile = pl.BlockSpec((1, TQ, D_MODEL), lambda i, j: (i, j, 0))
    k_blocks = [pl.BlockSpec((1, TQ, D_MODEL), lambda i, j, d=d: (i, (j + d) % nb, 0))
                for d in range(nb)]
    vt_blocks = [pl.BlockSpec((1, D_MODEL, TQ), lambda i, j, d=d: (i, 0, (j + d) % nb))
                 for d in range(nb)]
    return pl.pallas_call(
        functools.partial(_attn_c_kernel, lam_init, nb),
        grid=(b, nb),
        in_specs=[tile, tile, *k_blocks, *vt_blocks,
                  _resident((4, C_HEAD_DIM)), _resident((2 * C_HEAD_DIM, TQ)),
                  _resident((D_MODEL, D_MODEL))],
        out_specs=tile,
        out_shape=jax.ShapeDtypeStruct(x.shape, F32),
        scratch_shapes=[pltpu.VMEM((D_MODEL, TQ), F32)],
        compiler_params=_params("parallel", "parallel"),
        name="attn_c",
    )(x, q, *([k] * nb), *([vt] * nb), lam_params, sub_gain_t, wo)


F_EXTRA = BF16_ROWS


def _fourier_kernel(xd_ref, xm_ref, h_ref, ct_ref, st_ref, ctx_ref, stx_ref, cc_ref, sc_ref,
                    rev_ref, wo_ref, o_ref):
    h = h_ref[0]
    ct = jnp.concatenate([ct_ref[...], ctx_ref[...]], axis=0)
    st = jnp.concatenate([st_ref[...], stx_ref[...]], axis=0)
    p = _dot(ct, h).astype(BF16)
    q = _dot(st, h).astype(BF16)
    direct, mirror = [], []
    for g in range(F_GROUPS):
        lanes = slice(g * F_GROUP_DIM, (g + 1) * F_GROUP_DIM)
        a = _dot(p[:, lanes], cc_ref[...])
        b = _dot(q[:, lanes], sc_ref[...])
        direct.append(a[:TM] - b[:TM])
        mirror.append(a + b)
    f = jnp.concatenate(direct, axis=1).astype(BF16)
    o_ref[0, 0, 0] = xd_ref[0, 0] + _dot(f, wo_ref[...])
    fm = jnp.concatenate(mirror, axis=1).astype(BF16)
    rev = _dot(rev_ref[...], fm[:TM])
    row = lax.broadcasted_iota(jnp.int32, (TM, D_MODEL), 0)
    fm_rev = jnp.where(row == 0, fm[TM:TM + 1].astype(F32), rev).astype(BF16)
    o_ref[0, 0, 1] = xm_ref[0, 0] + _dot(fm_rev, wo_ref[...])


def _fourier(x, h, ct, st, cc, sc, rev, wo):
    b, seq, _ = x.shape
    nt = seq // TM
    assert nt == 4
    x4 = x.reshape(b, nt, TM, D_MODEL)
    tile = lambda f: pl.BlockSpec((1, 1, TM, D_MODEL), lambda i, j: (i, f(j), 0, 0))
    whole = pl.BlockSpec((1, seq, D_MODEL), lambda i, j: (i, 0, 0))
    tab = pl.BlockSpec((TM, seq), lambda i, j: (j, 0))
    tab_extra = pl.BlockSpec((F_EXTRA, seq), lambda i, j: ((j + 1) * (TM // F_EXTRA), 0))
    square = _resident((F_GROUP_DIM, F_GROUP_DIM))
    return pl.pallas_call(
        _fourier_kernel,
        grid=(b, nt // 2),
        in_specs=[tile(lambda j: j), tile(lambda j: nt - 1 - j), whole, tab, tab, tab_extra,
                  tab_extra, square, square, _resident((TM, TM)), _resident((D_MODEL, D_MODEL))],
        out_specs=pl.BlockSpec((1, 1, 2, TM, D_MODEL), lambda i, j: (i, j, 0, 0, 0)),
        out_shape=jax.ShapeDtypeStruct((b, nt // 2, 2, TM, D_MODEL), F32),
        compiler_params=_params("parallel", "parallel"),
        name="fourier",
    )(x4, x4, h, ct, st, ct, st, cc, sc, rev, wo)


def _fourier_slot(t):
    return jnp.where(t < 2, 2 * t, 7 - 2 * t)


def _rope_tables(seq):
    rows = seq // GRID_W
    r, c = jnp.meshgrid(jnp.arange(rows), jnp.arange(GRID_W), indexing="ij")
    r = r.reshape(-1).astype(F32)
    c = c.reshape(-1).astype(F32)
    n = A_HEAD_DIM // 4
    freqs = ROPE_THETA ** (-jnp.arange(n, dtype=F32) / n)
    ang = jnp.concatenate([r[:, None] * freqs, c[:, None] * freqs], axis=-1)
    cos = jnp.repeat(jnp.cos(ang), 2, axis=-1)
    sin = jnp.repeat(jnp.sin(ang), 2, axis=-1)
    sign = jnp.where(jnp.arange(A_HEAD_DIM) % 2 == 0, -1.0, 1.0).astype(F32)
    sin = sin * sign
    reps = LANES // A_HEAD_DIM
    return jnp.tile(cos, (1, reps)), jnp.tile(sin, (1, reps))


def _dft_tables(n, scale, rows=None):
    k = jnp.arange(rows or n, dtype=jnp.int32)
    kn = (k[:, None] * jnp.arange(n, dtype=jnp.int32)[None, :]) % n
    ang = kn.astype(F32) * (2.0 * math.pi / n)
    return (jnp.cos(ang) * scale).astype(BF16), (jnp.sin(ang) * scale).astype(BF16)


def _reversal_matrix():
    r = jnp.arange(TM)
    return (r[:, None] + r[None, :] == TM).astype(BF16)


def _head_reduce_tables():
    lane_head = jnp.arange(A_QK_W) // A_HEAD_DIM
    onehot = (lane_head[:, None] == jnp.arange(LANES)[None, :]).astype(F32)
    return (onehot / A_HEAD_DIM).astype(BF16), onehot.T.astype(BF16)


def _chunk_cols(w):
    return w.astype(BF16).reshape(D_MODEL, D_FF // FF_CHUNK, FF_CHUNK).transpose(1, 0, 2)


def _trunk(x, p):
    b, seq, _ = x.shape
    n = b * seq
    flat = lambda a: a.reshape(n, a.shape[-1])
    per_seq = lambda a: a.reshape(b, seq, a.shape[-1])
    x = flat(x)
    for i in LAYERS:
        kind = i % 4
        gain = p["attn_norm"][i][None]
        ffn_args = (p["ffn_norm"][i][None], p["w_gate"][i], p["w_up"][i], p["w_down"][i])
        if kind == 0:
            q, k, vt = _qkv_a(x, gain, p["a_w_qk"], p["a_w_vt"], p["e"], p["et"],
                              p["a_head_gain"], p["cos"], p["sin"], seq)
            x = flat(_attn_a(per_seq(x), per_seq(q), per_seq(k), vt, p["a_w_o"], *ffn_args))
            continue
        elif kind == 1:
            x = _pool(x, gain, p["b_w_pool"], p["b_scale"], seq, *ffn_args)
        elif kind == 2:
            lam_init = 0.8 - 0.6 * math.exp(-0.3 * i)
            q, k, vt = _qkv_c(x, gain, p["c_w_qk"], p["c_w_vt"], seq)
            x = flat(_attn_c(per_seq(x), per_seq(q), per_seq(k), vt, p["c_lambda"],
                             p["c_sub_gain_t"], p["c_w_o"], lam_init))
            x, h = _ffn(x, *ffn_args, p["attn_norm"][i + 1][None], "emit")
        else:
            assert i == DEPTH - 1
            x = _fourier(per_seq(x), per_seq(h), p["ct"], p["st"], p["cc"], p["sc"],
                         p["rev"], p["d_w_o"]).reshape(n, D_MODEL)
            tiles = seq // TM
            x = _ffn(x, *ffn_args, p["final_norm"], "final",
                     in_tile=lambda t: (t // tiles) * tiles + _fourier_slot(t % tiles))
    return per_seq(x)


def kernel(x_prompt, x_sample, attn_norm, ffn_norm, final_norm, a_w_qkv, a_q_gain, a_k_gain,
           a_w_o, b_w_pool, b_scale, c_w_qkv, c_lambda, c_sub_gain, c_w_o, d_w_o,
           w_gate, w_up, w_down):
    seq = x_prompt.shape[1]
    assert x_sample.shape[1] == seq and seq % TM == 0 and seq % TQ == 0
    e, et = _head_reduce_tables()
    cos, sin = _rope_tables(seq)
    ct, st = _dft_tables(seq, seq ** -0.5, rows=seq // 2 + F_EXTRA)
    cc, sc = _dft_tables(F_GROUP_DIM, F_GROUP_DIM ** -0.5)
    wa = a_w_qkv[0]
    wk = wa[:, A_Q_W:A_Q_W + A_V_W].reshape(D_MODEL, A_KV_HEADS, A_HEAD_DIM)
    a_w_qk = jnp.concatenate(
        [wa[:, :A_Q_W], jnp.concatenate([wk, wk], axis=-1).reshape(D_MODEL, A_KDUP_W)], axis=1)
    head_gain = jnp.concatenate([jnp.tile(a_q_gain[0], A_HEADS) * (A_HEAD_DIM ** -0.5 * LOG2E),
                                 jnp.tile(a_k_gain[0], 2 * A_KV_HEADS)])[None]
    wc = c_w_qkv[0]
    nc = D_FF // FF_CHUNK
    p = dict(
        attn_norm=attn_norm, ffn_norm=ffn_norm, final_norm=final_norm[None],
        a_w_qk=a_w_qk.astype(BF16), a_w_vt=wa[:, A_Q_W + A_V_W:].T.astype(BF16),
        a_head_gain=head_gain, a_w_o=a_w_o[0].astype(BF16),
        e=e, et=et, cos=cos, sin=sin,
        b_w_pool=b_w_pool[0].astype(BF16), b_scale=b_scale[0][None],
        c_w_qk=wc[:, :2 * D_MODEL].astype(BF16), c_w_vt=wc[:, 2 * D_MODEL:].T.astype(BF16),
        c_lambda=c_lambda[0],
        c_sub_gain_t=jnp.broadcast_to(c_sub_gain[0][:, None], (2 * C_HEAD_DIM, TQ)),
        c_w_o=c_w_o[0].astype(BF16),
        d_w_o=d_w_o[0].astype(BF16), ct=ct, st=st, cc=cc, sc=sc, rev=_reversal_matrix(),
        w_gate=[_chunk_cols(w_gate[i]) for i in range(DEPTH)],
        w_up=[_chunk_cols(w_up[i]) for i in range(DEPTH)],
        w_down=[w_down[i].astype(BF16).reshape(nc, FF_CHUNK, D_MODEL) for i in range(DEPTH)],
    )
    return _trunk(x_prompt, p), _trunk(x_sample, p)
```

```python
import functools
import math

import jax
import jax.numpy as jnp
import ml_dtypes
from jax import lax
from jax.experimental import pallas as pl
from jax.experimental.pallas import tpu as pltpu

F32 = jnp.float32
BF16 = jnp.bfloat16

D_MODEL = 1024
DEPTH = 4
LAYERS = tuple(range(DEPTH))
GRID_W = 64
NORM_EPS = 1e-6
A_HEADS = 16
A_KV_HEADS = 4
A_HEAD_DIM = 64
A_GROUP = A_HEADS // A_KV_HEADS
ROPE_THETA = 10000.0
POOL_WINDOWS = (2, 4, 8, 16)
POOL_GROUP = D_MODEL // len(POOL_WINDOWS)
POOL_HALO = 8
C_HEADS = 8
C_HEAD_DIM = 64
F_GROUPS = 4
F_GROUP_DIM = D_MODEL // F_GROUPS
D_FF = 2816
LOG2E = math.log2(math.e)


def _bf16_parts(x, n):
    parts = []
    for _ in range(n):
        parts.append(float(ml_dtypes.bfloat16(x)))
        x -= parts[-1]
    return tuple(parts)


assert C_HEADS == 8
ALIBI_E = 2.0 ** (-8.0 / C_HEADS) * LOG2E
ALIBI_E_PARTS = _bf16_parts(ALIBI_E, 3)

LANES = 128
BF16_ROWS = 16
VMEM_LIMIT = 56 * 1024 * 1024

A_Q_W = A_HEADS * A_HEAD_DIM
A_KDUP_W = A_KV_HEADS * LANES
A_QK_W = A_Q_W + A_KDUP_W
A_V_W = A_KV_HEADS * A_HEAD_DIM

TM = 512
TQ = 256
SUB_TILES = 2
OUT_DELAY = 2
FF_CHUNK = 256
KEY_SPLITS = 2


def _params(*sem):
    return pltpu.CompilerParams(dimension_semantics=sem, vmem_limit_bytes=VMEM_LIMIT)


def _resident(shape):
    nd = len(shape)
    return pl.BlockSpec(shape, lambda *_: (0,) * nd, pipeline_mode=pl.Buffered(1))


def _rms(x, gain):
    ms = jnp.mean(x * x, axis=-1, keepdims=True)
    return x * lax.rsqrt(ms + NORM_EPS) * gain


def _split_bf16(x):
    hi = x.astype(BF16)
    lo = (x - hi.astype(F32)).astype(BF16)
    return hi, lo


def _dot(a, b):
    return jnp.dot(a, b, preferred_element_type=F32)


def _dot_nt(a, b):
    return lax.dot_general(a, b, (((1,), (1,)), ((), ())), preferred_element_type=F32)


def _half_lanes_mask(rows, upper):
    lane = lax.broadcasted_iota(jnp.int32, (rows, LANES), 1)
    return lane >= LANES // 2 if upper else lane < LANES // 2


def _scores_t(k_halves, q, own_bias=None, own_row=0):
    halves = [_dot_nt(k, q) for k in k_halves]
    if own_bias is not None:
        first, end = halves[0], own_row + own_bias.shape[0]
        pieces = [first[:own_row], first[own_row:end] - own_bias, first[end:]]
        halves[0] = jnp.concatenate([piece for piece in pieces if piece.shape[0]], axis=0)
    m = functools.reduce(jnp.maximum, [jnp.max(s_t, axis=0, keepdims=True) for s_t in halves])
    return halves, m


def _probs_t(halves, m):
    return jnp.concatenate([jnp.exp2(s_t - m).astype(BF16) for s_t in halves], axis=0)


def _weighted_values_t(v_aug, p_t):
    d = v_aug.shape[0] - BF16_ROWS
    oa = _dot(v_aug, p_t)
    return oa[:d] / oa[d:d + 1]


def _pipelined(n, scores, probs, values, after=None):
    s = {0: scores(0)}
    if n > 1:
        s[1] = scores(1)
    p = {0: probs(*s.pop(0))}
    for i in range(n):
        if i + 2 < n:
            s[i + 2] = scores(i + 2)
        if i + 1 < n:
            p[i + 1] = probs(*s.pop(i + 1))
        values(i, p.pop(i))
        if after is not None:
            after(i)


def _project_late(n_sub, items_per_sub, project):
    done = []

    def after(i):
        for u in range(n_sub):
            ready = (u + 1) * items_per_sub - 1 + OUT_DELAY
            if u not in done and i >= min(ready, n_sub * items_per_sub - 1):
                done.append(u)
                project(u)
    return after


def _ffn_apply(x, g_ref, wg_ref, wu_ref, wd_ref):
    h = _rms(x, g_ref[...]).astype(BF16)
    acc = x
    for c in range(D_FF // FF_CHUNK):
        g = _dot(h, wg_ref[c])
        u = _dot(h, wu_ref[c])
        a = (g * jax.nn.sigmoid(g) * u).astype(BF16)
        acc = acc + _dot(a, wd_ref[c])
    return acc


def _ffn_specs():
    nc = D_FF // FF_CHUNK
    return [_resident((1, D_MODEL)), _resident((nc, D_MODEL, FF_CHUNK)),
            _resident((nc, D_MODEL, FF_CHUNK)), _resident((nc, FF_CHUNK, D_MODEL))]


def _ffn_kernel(mode, x_ref, g_ref, wg_ref, wu_ref, wd_ref, ng_ref, o_ref, *h_ref):
    acc = _ffn_apply(x_ref[...], g_ref, wg_ref, wu_ref, wd_ref)
    if mode == "plain":
        o_ref[...] = acc
        return
    normed = _rms(acc, ng_ref[...])
    o_ref[...] = normed if mode == "final" else acc
    if mode == "emit":
        h_ref[0][...] = normed.astype(BF16)


def _ffn(x, gain, wg, wu, wd, next_gain, mode, in_tile=lambda i: i):
    n = x.shape[0]
    row = pl.BlockSpec((TM, D_MODEL), lambda i: (i, 0))
    out = jax.ShapeDtypeStruct((n, D_MODEL), F32)
    emit = mode == "emit"
    return pl.pallas_call(
        functools.partial(_ffn_kernel, mode),
        grid=(n // TM,),
        in_specs=[pl.BlockSpec((TM, D_MODEL), lambda i: (in_tile(i), 0)), *_ffn_specs(),
                  _resident((1, D_MODEL))],
        out_specs=[row, row] if emit else row,
        out_shape=[out, jax.ShapeDtypeStruct((n, D_MODEL), BF16)] if emit else out,
        compiler_params=_params("parallel"),
        name="ffn_" + mode,
    )(x, gain, wg, wu, wd, next_gain)


def _qkv_a_kernel(x_ref, g_ref, w_ref, wvt_ref, e_ref, et_ref, hg_ref, cos_ref, sin_ref,
                  q_ref, k_ref, vt_ref):
    h = _rms(x_ref[...], g_ref[...]).astype(BF16)
    qk = _dot(h, w_ref[...])
    sq_hi, sq_lo = _split_bf16(qk * qk)
    ms = _dot(sq_hi, e_ref[...]) + _dot(sq_lo, e_ref[...])
    rs_hi, rs_lo = _split_bf16(lax.rsqrt(ms + NORM_EPS))
    rs = _dot(rs_hi, et_ref[...]) + _dot(rs_lo, et_ref[...])
    qn = qk * rs * hg_ref[...]
    lane = lax.broadcasted_iota(jnp.int32, qn.shape, 1)
    partner = jnp.where(lane % 2 == 0, pltpu.roll(qn, A_QK_W - 1, 1), pltpu.roll(qn, 1, 1))
    reps = A_QK_W // LANES
    cos = jnp.concatenate([cos_ref[...]] * reps, axis=1)
    sin = jnp.concatenate([sin_ref[...]] * reps, axis=1)
    out = qn * cos + partner * sin
    q_ref[...] = out[:, :A_Q_W].astype(BF16)
    k_ref[...] = out[:, A_Q_W:].astype(BF16)
    vt_ref[0] = _dot_nt(wvt_ref[...], h).astype(BF16)


def _qkv_a(x, gain, w, wvt, e, et, head_gain, cos, sin, seq):
    n = x.shape[0]
    spt = seq // TM
    row = lambda width: pl.BlockSpec((TM, width), lambda i: (i, 0))
    tab = pl.BlockSpec((TM, LANES), lambda i: (i % spt, 0))
    return pl.pallas_call(
        _qkv_a_kernel,
        grid=(n // TM,),
        in_specs=[row(D_MODEL), _resident((1, D_MODEL)), _resident((D_MODEL, A_QK_W)),
                  _resident((A_V_W, D_MODEL)), _resident((A_QK_W, LANES)),
                  _resident((LANES, A_QK_W)), _resident((1, A_QK_W)), tab, tab],
        out_specs=[row(A_Q_W), row(A_KDUP_W),
                   pl.BlockSpec((1, A_V_W, TM), lambda i: (i // spt, 0, i % spt))],
        out_shape=[jax.ShapeDtypeStruct((n, A_Q_W), BF16),
                   jax.ShapeDtypeStruct((n, A_KDUP_W), BF16),
                   jax.ShapeDtypeStruct((n // seq, A_V_W, seq), BF16)],
        compiler_params=_params("parallel"),
        name="qkv_a",
    )(x, gain, w, wvt, e, et, head_gain, cos, sin)


def _project_out(u, x_ref, ot_scr, wo_ref, o_ref):
    rows = slice(u * TQ, (u + 1) * TQ)
    o = ot_scr[u].T.astype(BF16)
    o_ref[0, rows] = x_ref[0, rows] + _dot(o, wo_ref[...])


def _attn_a_kernel(x_ref, q_ref, k_ref, vt_ref, wo_ref, o_ref, ot_scr):
    seq = k_ref.shape[1]
    hd = A_HEAD_DIM
    ones = jnp.ones((BF16_ROWS, seq), BF16)

    def scores(i):
        u, h = divmod(i, A_HEADS)
        g = h // A_GROUP
        q_pair = q_ref[0, u * TQ:(u + 1) * TQ, (h // 2) * LANES:(h // 2 + 1) * LANES]
        q_h = jnp.where(_half_lanes_mask(TQ, h % 2 == 1), q_pair, jnp.zeros_like(q_pair))
        part = seq // KEY_SPLITS
        k_halves = [k_ref[0, r * part:(r + 1) * part, g * LANES:(g + 1) * LANES]
                    for r in range(KEY_SPLITS)]
        return _scores_t(k_halves, q_h)

    def values(i, p_t):
        u, h = divmod(i, A_HEADS)
        g = h // A_GROUP
        v_aug = jnp.concatenate([vt_ref[0, g * hd:(g + 1) * hd, :], ones], axis=0)
        ot_scr[u, h * hd:(h + 1) * hd, :] = _weighted_values_t(v_aug, p_t)

    project = functools.partial(_project_out, x_ref=x_ref, ot_scr=ot_scr, wo_ref=wo_ref,
                                o_ref=o_ref)
    _pipelined(SUB_TILES * A_HEADS, scores, _probs_t, values,
               _project_late(SUB_TILES, A_HEADS, project))


def _attn_a(x, q, k, vt, wo):
    b, seq, _ = x.shape
    rows = SUB_TILES * TQ
    tile = pl.BlockSpec((1, rows, D_MODEL), lambda i, j: (i, j, 0))
    return pl.pallas_call(
        _attn_a_kernel,
        grid=(b, seq // rows),
        in_specs=[tile, tile,
                  pl.BlockSpec((1, seq, A_KDUP_W), lambda i, j: (i, 0, 0)),
                  pl.BlockSpec((1, A_V_W, seq), lambda i, j: (i, 0, 0)),
                  _resident((D_MODEL, D_MODEL))],
        out_specs=tile,
        out_shape=jax.ShapeDtypeStruct(x.shape, F32),
        scratch_shapes=[pltpu.VMEM((SUB_TILES, D_MODEL, TQ), F32)],
        compiler_params=_params("parallel", "parallel"),
        name="attn_a",
    )(x, q, k, vt, wo)


def _pool_kernel(seq, x_ref, prev_ref, next_ref, g_ref, w_ref, sc_ref, o_ref, h_scr):
    i = pl.program_id(0)
    t0 = (i % (seq // TM)) * TM
    gain = g_ref[...]
    x = x_ref[...]
    h = _rms(x, gain)
    h_scr[0:POOL_HALO, :] = jnp.where(t0 > 0, _rms(prev_ref[...], gain), 0.0)
    h_scr[POOL_HALO:POOL_HALO + TM, :] = h
    h_scr[POOL_HALO + TM:, :] = jnp.where(t0 + TM < seq, _rms(next_ref[...], gain), 0.0)
    t = t0 + lax.broadcasted_iota(jnp.int32, (TM, 1), 0)
    ys = []
    for g, win in enumerate(POOL_WINDOWS):
        lanes = slice(g * POOL_GROUP, (g + 1) * POOL_GROUP)
        half = win // 2
        total = h_scr[POOL_HALO - half:POOL_HALO - half + TM, lanes]
        for d in range(1, win):
            total = total + h_scr[POOL_HALO - half + d:POOL_HALO - half + d + TM, lanes]
        lo = jnp.clip(t - half, 0, seq - 1)
        hi = jnp.clip(t + half - 1, 0, seq - 1)
        cnt = (hi - lo + 1).astype(F32)
        p = (total / cnt - h[:, lanes]).astype(BF16)
        ys.append(_dot(p, w_ref[g]))
    o_ref[...] = x + jnp.concatenate(ys, axis=1) * sc_ref[...]


def _pool(x, gain, w, scale, seq):
    n = x.shape[0]
    halo_per_tile = TM // POOL_HALO
    last_halo = n // POOL_HALO - 1
    row = pl.BlockSpec((TM, D_MODEL), lambda i: (i, 0))
    prev = pl.BlockSpec((POOL_HALO, D_MODEL),
                        lambda i: (jnp.maximum(i * halo_per_tile - 1, 0), 0))
    nxt = pl.BlockSpec((POOL_HALO, D_MODEL),
                       lambda i: (jnp.minimum((i + 1) * halo_per_tile, last_halo), 0))
    ng = len(POOL_WINDOWS)
    return pl.pallas_call(
        functools.partial(_pool_kernel, seq),
        grid=(n // TM,),
        in_specs=[row, prev, nxt, _resident((1, D_MODEL)),
                  _resident((ng, POOL_GROUP, POOL_GROUP)), _resident((1, D_MODEL))],
        out_specs=row,
        out_shape=jax.ShapeDtypeStruct((n, D_MODEL), F32),
        scratch_shapes=[pltpu.VMEM((TM + 2 * POOL_HALO, D_MODEL), F32)],
        compiler_params=_params("parallel"),
        name="pool",
    )(x, x, x, gain, w, scale)


def _qkv_c_kernel(x_ref, g_ref, w_ref, wvt_ref, q_ref, k_ref, vt_ref):
    h = _rms(x_ref[...], g_ref[...]).astype(BF16)
    width = 2 * C_HEADS * C_HEAD_DIM
    qk = _dot(h, w_ref[...])
    q_ref[...] = (qk[:, :width] * (C_HEAD_DIM ** -0.5 * LOG2E)).astype(BF16)
    k_ref[...] = qk[:, width:].astype(BF16)
    vt_ref[0] = _dot_nt(wvt_ref[...], h).astype(BF16)


def _qkv_c(x, gain, w, wvt, seq):
    n = x.shape[0]
    spt = seq // TM
    row = pl.BlockSpec((TM, D_MODEL), lambda i: (i, 0))
    out = jax.ShapeDtypeStruct((n, D_MODEL), BF16)
    return pl.pallas_call(
        _qkv_c_kernel,
        grid=(n // TM,),
        in_specs=[row, _resident((1, D_MODEL)), _resident((D_MODEL, 2 * D_MODEL)),
                  _resident((D_MODEL, D_MODEL))],
        out_specs=[row, row, pl.BlockSpec((1, D_MODEL, TM), lambda i: (i // spt, 0, i % spt))],
        out_shape=[out, out, jax.ShapeDtypeStruct((n // seq, D_MODEL, seq), BF16)],
        compiler_params=_params("parallel"),
        name="qkv_c",
    )(x, gain, w, wvt)


def _lane_pattern(lane, values):
    out = jnp.zeros(lane.shape, F32)
    for l, v in enumerate(values):
        out = jnp.where(lane == l, v, out)
    return out


def _attn_c_kernel(lam_init, nb, x_ref, q_ref, *refs):
    k_refs, vt_refs = refs[:nb], refs[nb:2 * nb]
    lp_ref, sg_ref, wo_ref, o_ref, ot_scr = refs[2 * nb:]
    hw = 2 * C_HEAD_DIM
    first = pl.program_id(1) * SUB_TILES
    lp = lp_ref[...]
    lam = (jnp.exp(jnp.sum(lp[0:1] * lp[1:2], axis=-1, keepdims=True))
           - jnp.exp(jnp.sum(lp[2:3] * lp[3:4], axis=-1, keepdims=True)) + lam_init)

    lane = lax.broadcasted_iota(jnp.int32, (TQ, LANES), 1)
    offset = lax.broadcasted_iota(jnp.int32, (TQ, LANES), 0).astype(F32)
    e_parts = list(ALIBI_E_PARTS)
    k_aug = [[None] * nb for _ in range(SUB_TILES)]
    for d in range(nb):
        j0 = (((first + d) % nb) * TQ).astype(F32)
        pattern = _lane_pattern(lane, 2 * e_parts + 3 * [j0] + 3 * [offset])
        wrapped = first + d >= nb
        for u in range(SUB_TILES):
            if d == u:
                k_aug[u][d] = jnp.zeros((TQ, LANES), BF16)
            else:
                sigma = 1.0 if d < u else jnp.where(wrapped, 1.0, -1.0)
                k_aug[u][d] = (sigma * pattern).astype(BF16)
    q_aug = []
    for u in range(SUB_TILES):
        i0 = ((first + u) * TQ).astype(F32)
        q_aug0 = _lane_pattern(lane, 3 * [-offset] + 3 * [-i0] + 2 * e_parts)
        q_aug.append([(q_aug0 * 2.0 ** -h).astype(BF16) for h in range(C_HEADS)])
    own_dist = jnp.abs(lax.broadcasted_iota(jnp.int32, (TQ, TQ), 0)
                       - lax.broadcasted_iota(jnp.int32, (TQ, TQ), 1)).astype(F32)
    ones = jnp.ones((BF16_ROWS, nb * TQ), BF16)
    os = [None, None]
    per_half = nb // KEY_SPLITS
    assert SUB_TILES <= per_half

    def scores(i):
        u, rest = divmod(i, 2 * C_HEADS)
        h, c = divmod(rest, 2)
        lanes = slice(h * hw, (h + 1) * hw)
        q_pair = q_ref[0, u * TQ:(u + 1) * TQ, lanes]
        q_c = jnp.where(_half_lanes_mask(TQ, c == 1), q_pair, jnp.zeros_like(q_pair))
        q_full = jnp.concatenate([q_c, q_aug[u][h]], axis=1)
        k_halves = [
            jnp.concatenate(
                [jnp.concatenate([k_refs[d][0, :, lanes], k_aug[u][d]], axis=1)
                 for d in range(r * per_half, (r + 1) * per_half)], axis=0)
            for r in range(KEY_SPLITS)]
        return _scores_t(k_halves, q_full, own_dist * (ALIBI_E * 2.0 ** -h), u * TQ)

    def values(i, p_t):
        u, rest = divmod(i, 2 * C_HEADS)
        h, c = divmod(rest, 2)
        v_t = jnp.concatenate([vt_refs[d][0, h * hw:(h + 1) * hw, :] for d in range(nb)], axis=1)
        os[c] = _weighted_values_t(jnp.concatenate([v_t, ones], axis=0), p_t)
        if c == 1:
            o = os[0] - lam * os[1]
            ms = jnp.mean(o * o, axis=0, keepdims=True)
            ot_scr[u, h * hw:(h + 1) * hw, :] = (
                o * lax.rsqrt(ms + NORM_EPS) * sg_ref[...] * (1.0 - lam_init))

    project = functools.partial(_project_out, x_ref=x_ref, ot_scr=ot_scr, wo_ref=wo_ref,
                                o_ref=o_ref)
    _pipelined(SUB_TILES * 2 * C_HEADS, scores, _probs_t, values,
               _project_late(SUB_TILES, 2 * C_HEADS, project))


def _attn_c(x, q, k, vt, lam_params, sub_gain_t, wo, lam_init):
    b, seq, _ = x.shape
    nb = seq // TQ
    rows = SUB_TILES * TQ
    tile = pl.BlockSpec((1, rows, D_MODEL), lambda i, j: (i, j, 0))
    k_blocks = [pl.BlockSpec((1, TQ, D_MODEL),
                             lambda i, j, d=d: (i, (j * SUB_TILES + d) % nb, 0))
                for d in range(nb)]
    vt_blocks = [pl.BlockSpec((1, D_MODEL, TQ),
                              lambda i, j, d=d: (i, 0, (j * SUB_TILES + d) % nb))
                 for d in range(nb)]
    return pl.pallas_call(
        functools.partial(_attn_c_kernel, lam_init, nb),
        grid=(b, seq // rows),
        in_specs=[tile, tile, *k_blocks, *vt_blocks,
                  _resident((4, C_HEAD_DIM)), _resident((2 * C_HEAD_DIM, TQ)),
                  _resident((D_MODEL, D_MODEL))],
        out_specs=tile,
        out_shape=jax.ShapeDtypeStruct(x.shape, F32),
        scratch_shapes=[pltpu.VMEM((SUB_TILES, D_MODEL, TQ), F32)],
        compiler_params=_params("parallel", "parallel"),
        name="attn_c",
    )(x, q, *([k] * nb), *([vt] * nb), lam_params, sub_gain_t, wo)


F_EXTRA = BF16_ROWS


def _fourier_kernel(xd_ref, xm_ref, h_ref, ct_ref, st_ref, ctx_ref, stx_ref, cc_ref, sc_ref,
                    rev_ref, wo_ref, o_ref):
    h = h_ref[0]
    ct = jnp.concatenate([ct_ref[...], ctx_ref[...]], axis=0)
    st = jnp.concatenate([st_ref[...], stx_ref[...]], axis=0)
    p = _dot(ct, h).astype(BF16)
    q = _dot(st, h).astype(BF16)
    direct, mirror = [], []
    for g in range(F_GROUPS):
        lanes = slice(g * F_GROUP_DIM, (g + 1) * F_GROUP_DIM)
        a = _dot(p[:, lanes], cc_ref[...])
        b = _dot(q[:, lanes], sc_ref[...])
        direct.append(a[:TM] - b[:TM])
        mirror.append(a + b)
    f = jnp.concatenate(direct, axis=1).astype(BF16)
    o_ref[0, 0, 0] = xd_ref[0, 0] + _dot(f, wo_ref[...])
    fm = jnp.concatenate(mirror, axis=1).astype(BF16)
    rev = _dot(rev_ref[...], fm[:TM])
    row = lax.broadcasted_iota(jnp.int32, (TM, D_MODEL), 0)
    fm_rev = jnp.where(row == 0, fm[TM:TM + 1].astype(F32), rev).astype(BF16)
    o_ref[0, 0, 1] = xm_ref[0, 0] + _dot(fm_rev, wo_ref[...])


def _fourier(x, h, ct, st, cc, sc, rev, wo):
    b, seq, _ = x.shape
    nt = seq // TM
    assert nt == 4
    x4 = x.reshape(b, nt, TM, D_MODEL)
    tile = lambda f: pl.BlockSpec((1, 1, TM, D_MODEL), lambda i, j: (i, f(j), 0, 0))
    whole = pl.BlockSpec((1, seq, D_MODEL), lambda i, j: (i, 0, 0))
    tab = pl.BlockSpec((TM, seq), lambda i, j: (j, 0))
    tab_extra = pl.BlockSpec((F_EXTRA, seq), lambda i, j: ((j + 1) * (TM // F_EXTRA), 0))
    square = _resident((F_GROUP_DIM, F_GROUP_DIM))
    return pl.pallas_call(
        _fourier_kernel,
        grid=(b, nt // 2),
        in_specs=[tile(lambda j: j), tile(lambda j: nt - 1 - j), whole, tab, tab, tab_extra,
                  tab_extra, square, square, _resident((TM, TM)), _resident((D_MODEL, D_MODEL))],
        out_specs=pl.BlockSpec((1, 1, 2, TM, D_MODEL), lambda i, j: (i, j, 0, 0, 0)),
        out_shape=jax.ShapeDtypeStruct((b, nt // 2, 2, TM, D_MODEL), F32),
        compiler_params=_params("parallel", "parallel"),
        name="fourier",
    )(x4, x4, h, ct, st, ct, st, cc, sc, rev, wo)


def _fourier_slot(t):
    return jnp.where(t < 2, 2 * t, 7 - 2 * t)


def _rope_tables(seq):
    rows = seq // GRID_W
    r, c = jnp.meshgrid(jnp.arange(rows), jnp.arange(GRID_W), indexing="ij")
    r = r.reshape(-1).astype(F32)
    c = c.reshape(-1).astype(F32)
    n = A_HEAD_DIM // 4
    freqs = ROPE_THETA ** (-jnp.arange(n, dtype=F32) / n)
    ang = jnp.concatenate([r[:, None] * freqs, c[:, None] * freqs], axis=-1)
    cos = jnp.repeat(jnp.cos(ang), 2, axis=-1)
    sin = jnp.repeat(jnp.sin(ang), 2, axis=-1)
    sign = jnp.where(jnp.arange(A_HEAD_DIM) % 2 == 0, -1.0, 1.0).astype(F32)
    sin = sin * sign
    reps = LANES // A_HEAD_DIM
    return jnp.tile(cos, (1, reps)), jnp.tile(sin, (1, reps))


def _dft_tables(n, scale, rows=None):
    k = jnp.arange(rows or n, dtype=jnp.int32)
    kn = (k[:, None] * jnp.arange(n, dtype=jnp.int32)[None, :]) % n
    ang = kn.astype(F32) * (2.0 * math.pi / n)
    return (jnp.cos(ang) * scale).astype(BF16), (jnp.sin(ang) * scale).astype(BF16)


def _reversal_matrix():
    r = jnp.arange(TM)
    return (r[:, None] + r[None, :] == TM).astype(BF16)


def _head_reduce_tables():
    lane_head = jnp.arange(A_QK_W) // A_HEAD_DIM
    onehot = (lane_head[:, None] == jnp.arange(LANES)[None, :]).astype(F32)
    return (onehot / A_HEAD_DIM).astype(BF16), onehot.T.astype(BF16)


def _chunk_cols(w):
    return w.astype(BF16).reshape(D_MODEL, D_FF // FF_CHUNK, FF_CHUNK).transpose(1, 0, 2)


def _trunk(x, p):
    b, seq, _ = x.shape
    n = b * seq
    flat = lambda a: a.reshape(n, a.shape[-1])
    per_seq = lambda a: a.reshape(b, seq, a.shape[-1])
    x = flat(x)
    for i in LAYERS:
        kind = i % 4
        gain = p["attn_norm"][i][None]
        ffn_args = (p["ffn_norm"][i][None], p["w_gate"][i], p["w_up"][i], p["w_down"][i])
        if kind == 0:
            q, k, vt = _qkv_a(x, gain, p["a_w_qk"], p["a_w_vt"], p["e"], p["et"],
                              p["a_head_gain"], p["cos"], p["sin"], seq)
            x = flat(_attn_a(per_seq(x), per_seq(q), per_seq(k), vt, p["a_w_o"]))
            x = _ffn(x, *ffn_args, p["final_norm"], "plain")
        elif kind == 1:
            x = _pool(x, gain, p["b_w_pool"], p["b_scale"], seq)
            x = _ffn(x, *ffn_args, p["final_norm"], "plain")
        elif kind == 2:
            lam_init = 0.8 - 0.6 * math.exp(-0.3 * i)
            q, k, vt = _qkv_c(x, gain, p["c_w_qk"], p["c_w_vt"], seq)
            x = flat(_attn_c(per_seq(x), per_seq(q), per_seq(k), vt, p["c_lambda"],
                             p["c_sub_gain_t"], p["c_w_o"], lam_init))
            x, h = _ffn(x, *ffn_args, p["attn_norm"][i + 1][None], "emit")
        else:
            assert i == DEPTH - 1
            x = _fourier(per_seq(x), per_seq(h), p["ct"], p["st"], p["cc"], p["sc"],
                         p["rev"], p["d_w_o"]).reshape(n, D_MODEL)
            tiles = seq // TM
            x = _ffn(x, *ffn_args, p["final_norm"], "final",
                     in_tile=lambda t: (t // tiles) * tiles + _fourier_slot(t % tiles))
    return per_seq(x)


def kernel(x_prompt, x_sample, attn_norm, ffn_norm, final_norm, a_w_qkv, a_q_gain, a_k_gain,
           a_w_o, b_w_pool, b_scale, c_w_qkv, c_lambda, c_sub_gain, c_w_o, d_w_o,
           w_gate, w_up, w_down):
    seq = x_prompt.shape[1]
    assert x_sample.shape[1] == seq and seq % TM == 0 and seq % (SUB_TILES * TQ) == 0
    e, et = _head_reduce_tables()
    cos, sin = _rope_tables(seq)
    ct, st = _dft_tables(seq, seq ** -0.5, rows=seq // 2 + F_EXTRA)
    cc, sc = _dft_tables(F_GROUP_DIM, F_GROUP_DIM ** -0.5)
    wa = a_w_qkv[0]
    wk = wa[:, A_Q_W:A_Q_W + A_V_W].reshape(D_MODEL, A_KV_HEADS, A_HEAD_DIM)
    a_w_qk = jnp.concatenate(
        [wa[:, :A_Q_W], jnp.concatenate([wk, wk], axis=-1).reshape(D_MODEL, A_KDUP_W)], axis=1)
    head_gain = jnp.concatenate([jnp.tile(a_q_gain[0], A_HEADS) * (A_HEAD_DIM ** -0.5 * LOG2E),
                                 jnp.tile(a_k_gain[0], 2 * A_KV_HEADS)])[None]
    wc = c_w_qkv[0]
    nc = D_FF // FF_CHUNK
    p = dict(
        attn_norm=attn_norm, ffn_norm=ffn_norm, final_norm=final_norm[None],
        a_w_qk=a_w_qk.astype(BF16), a_w_vt=wa[:, A_Q_W + A_V_W:].T.astype(BF16),
        a_head_gain=head_gain, a_w_o=a_w_o[0].astype(BF16),
        e=e, et=et, cos=cos, sin=sin,
        b_w_pool=b_w_pool[0].astype(BF16), b_scale=b_scale[0][None],
        c_w_qk=wc[:, :2 * D_MODEL].astype(BF16), c_w_vt=wc[:, 2 * D_MODEL:].T.astype(BF16),
        c_lambda=c_lambda[0],
        c_sub_gain_t=jnp.broadcast_to(c_sub_gain[0][:, None], (2 * C_HEAD_DIM, TQ)),
        c_w_o=c_w_o[0].astype(BF16),
        d_w_o=d_w_o[0].astype(BF16), ct=ct, st=st, cc=cc, sc=sc, rev=_reversal_matrix(),
        w_gate=[_chunk_cols(w_gate[i]) for i in range(DEPTH)],
        w_up=[_chunk_cols(w_up[i]) for i in range(DEPTH)],
        w_down=[w_down[i].astype(BF16).reshape(nc, FF_CHUNK, D_MODEL) for i in range(DEPTH)],
    )
    return _trunk(x_prompt, p), _trunk(x_sample, p)
```

```python
import functools
import math

import jax
import jax.numpy as jnp
import ml_dtypes
from jax import lax
from jax.experimental import pallas as pl
from jax.experimental.pallas import tpu as pltpu

F32 = jnp.float32
BF16 = jnp.bfloat16

D_MODEL = 1024
DEPTH = 4
LAYERS = tuple(range(DEPTH))
GRID_W = 64
NORM_EPS = 1e-6
A_HEADS = 16
A_KV_HEADS = 4
A_HEAD_DIM = 64
A_GROUP = A_HEADS // A_KV_HEADS
ROPE_THETA = 10000.0
POOL_WINDOWS = (2, 4, 8, 16)
POOL_GROUP = D_MODEL // len(POOL_WINDOWS)
POOL_HALO = 8
C_HEADS = 8
C_HEAD_DIM = 64
F_GROUPS = 4
F_GROUP_DIM = D_MODEL // F_GROUPS
D_FF = 2816
LOG2E = math.log2(math.e)


def _bf16_parts(x, n):
    parts = []
    for _ in range(n):
        parts.append(float(ml_dtypes.bfloat16(x)))
        x -= parts[-1]
    return tuple(parts)


assert C_HEADS == 8
ALIBI_E = 2.0 ** (-8.0 / C_HEADS) * LOG2E
ALIBI_E_PARTS = _bf16_parts(ALIBI_E, 3)

LANES = 128
BF16_ROWS = 16
VMEM_LIMIT = 56 * 1024 * 1024

A_Q_W = A_HEADS * A_HEAD_DIM
A_V_W = A_KV_HEADS * A_HEAD_DIM
A_QK_W = A_Q_W + A_V_W
A_KDUP_W = A_KV_HEADS * LANES

TM = 512
TQ = 256
SUB_TILES = 2
OUT_DELAY = 2
FF_CHUNK = 256
KEY_SPLITS = 2


def _params(*sem):
    return pltpu.CompilerParams(dimension_semantics=sem, vmem_limit_bytes=VMEM_LIMIT)


def _resident(shape):
    nd = len(shape)
    return pl.BlockSpec(shape, lambda *_: (0,) * nd, pipeline_mode=pl.Buffered(1))


def _rms(x, gain):
    ms = jnp.mean(x * x, axis=-1, keepdims=True)
    return x * lax.rsqrt(ms + NORM_EPS) * gain


def _split_bf16(x):
    hi = x.astype(BF16)
    lo = (x - hi.astype(F32)).astype(BF16)
    return hi, lo


def _dot(a, b):
    return jnp.dot(a, b, preferred_element_type=F32)


def _dot_nt(a, b):
    return lax.dot_general(a, b, (((1,), (1,)), ((), ())), preferred_element_type=F32)


def _half_lanes_mask(rows, upper):
    lane = lax.broadcasted_iota(jnp.int32, (rows, LANES), 1)
    return lane >= LANES // 2 if upper else lane < LANES // 2


def _scores_t(k_halves, q, own_bias=None, own_row=0):
    halves = [_dot_nt(k, q) for k in k_halves]
    if own_bias is not None:
        first, end = halves[0], own_row + own_bias.shape[0]
        pieces = [first[:own_row], first[own_row:end] - own_bias, first[end:]]
        halves[0] = jnp.concatenate([piece for piece in pieces if piece.shape[0]], axis=0)
    m = functools.reduce(jnp.maximum, [jnp.max(s_t, axis=0, keepdims=True) for s_t in halves])
    return halves, m


S_SLOTS = 3
P_SLOTS = 2


def _stage_scratch(keys):
    return [pltpu.VMEM((S_SLOTS, keys, TQ), F32), pltpu.VMEM((P_SLOTS, keys, TQ), BF16)]


def _park_scores(s_scr, i, halves, m):
    part = halves[0].shape[0]
    for r, s_t in enumerate(halves):
        s_scr[i % s_scr.shape[0], r * part:(r + 1) * part, :] = s_t
    return i, m


def _probs_t(s_scr, p_scr, i, m):
    p_scr[i % p_scr.shape[0]] = jnp.exp2(s_scr[i % s_scr.shape[0]] - m).astype(BF16)
    return i


def _weighted_values_t(v_aug, p_t):
    d = v_aug.shape[0] - BF16_ROWS
    oa = _dot(v_aug, p_t)
    return oa[:d] / oa[d:d + 1]


def _pipelined(n, scores, probs, values, after=None):
    s = {0: scores(0)}
    if n > 1:
        s[1] = scores(1)
    p = {0: probs(*s.pop(0))}
    for i in range(n):
        if i + 2 < n:
            s[i + 2] = scores(i + 2)
        if i + 1 < n:
            p[i + 1] = probs(*s.pop(i + 1))
        values(i, p.pop(i))
        if after is not None:
            after(i)


def _project_late(n_sub, items_per_sub, project):
    done = []

    def after(i):
        for u in range(n_sub):
            ready = (u + 1) * items_per_sub - 1 + OUT_DELAY
            if u not in done and i >= min(ready, n_sub * items_per_sub - 1):
                done.append(u)
                project(u)
    return after


def _ffn_apply(x, g_ref, wg_ref, wu_ref, wd_ref):
    h = _rms(x, g_ref[...]).astype(BF16)
    acc = x
    for c in range(D_FF // FF_CHUNK):
        g = _dot(h, wg_ref[c])
        u = _dot(h, wu_ref[c])
        a = (g * jax.nn.sigmoid(g) * u).astype(BF16)
        acc = acc + _dot(a, wd_ref[c])
    return acc


def _ffn_specs():
    nc = D_FF // FF_CHUNK
    return [_resident((1, D_MODEL)), _resident((nc, D_MODEL, FF_CHUNK)),
            _resident((nc, D_MODEL, FF_CHUNK)), _resident((nc, FF_CHUNK, D_MODEL))]


def _ffn_kernel(mode, x_ref, g_ref, wg_ref, wu_ref, wd_ref, ng_ref, o_ref, *h_ref):
    acc = _ffn_apply(x_ref[...], g_ref, wg_ref, wu_ref, wd_ref)
    if mode == "plain":
        o_ref[...] = acc
        return
    normed = _rms(acc, ng_ref[...])
    o_ref[...] = normed if mode == "final" else acc
    if mode == "emit":
        h_ref[0][...] = normed.astype(BF16)


def _ffn(x, gain, wg, wu, wd, next_gain, mode, in_tile=lambda i: i):
    n = x.shape[0]
    row = pl.BlockSpec((TM, D_MODEL), lambda i: (i, 0))
    out = jax.ShapeDtypeStruct((n, D_MODEL), F32)
    emit = mode == "emit"
    return pl.pallas_call(
        functools.partial(_ffn_kernel, mode),
        grid=(n // TM,),
        in_specs=[pl.BlockSpec((TM, D_MODEL), lambda i: (in_tile(i), 0)), *_ffn_specs(),
                  _resident((1, D_MODEL))],
        out_specs=[row, row] if emit else row,
        out_shape=[out, jax.ShapeDtypeStruct((n, D_MODEL), BF16)] if emit else out,
        compiler_params=_params("parallel"),
        name="ffn_" + mode,
    )(x, gain, wg, wu, wd, next_gain)


def _qkv_a_kernel(x_ref, g_ref, w_ref, wvt_ref, e_ref, et_ref, hg_ref, cos_ref, sin_ref,
                  q_ref, k_ref, vt_ref):
    h = _rms(x_ref[...], g_ref[...]).astype(BF16)
    qk = _dot(h, w_ref[...])
    sq_hi, sq_lo = _split_bf16(qk * qk)
    ms = _dot(sq_hi, e_ref[...]) + _dot(sq_lo, e_ref[...])
    rs = _dot(jnp.concatenate(_split_bf16(lax.rsqrt(ms + NORM_EPS)), axis=1), et_ref[...])
    qn = qk * rs * hg_ref[...]
    lane = lax.broadcasted_iota(jnp.int32, qn.shape, 1)
    partner = jnp.where(lane % 2 == 0, pltpu.roll(qn, A_QK_W - 1, 1), pltpu.roll(qn, 1, 1))
    reps = A_QK_W // LANES
    cos = jnp.concatenate([cos_ref[...]] * reps, axis=1)
    sin = jnp.concatenate([sin_ref[...]] * reps, axis=1)
    out = qn * cos + partner * sin
    q_ref[...] = out[:, :A_Q_W].astype(BF16)
    kk = out[:, A_Q_W:]
    up = pltpu.roll(kk, A_HEAD_DIM, 1)
    down = pltpu.roll(kk, A_V_W - A_HEAD_DIM, 1)
    lower = _half_lanes_mask(kk.shape[0], False)
    for g in range(A_KV_HEADS):
        src = slice((g // 2) * LANES, (g // 2 + 1) * LANES)
        pair = (jnp.where(lower, kk[:, src], up[:, src]) if g % 2 == 0
                else jnp.where(lower, down[:, src], kk[:, src]))
        k_ref[:, g * LANES:(g + 1) * LANES] = pair.astype(BF16)
    vt_ref[0] = _dot_nt(wvt_ref[...], h).astype(BF16)


def _qkv_a(x, gain, w, wvt, e, et, head_gain, cos, sin, seq):
    n = x.shape[0]
    spt = seq // TM
    row = lambda width: pl.BlockSpec((TM, width), lambda i: (i, 0))
    tab = pl.BlockSpec((TM, LANES), lambda i: (i % spt, 0))
    return pl.pallas_call(
        _qkv_a_kernel,
        grid=(n // TM,),
        in_specs=[row(D_MODEL), _resident((1, D_MODEL)), _resident((D_MODEL, A_QK_W)),
                  _resident((A_V_W, D_MODEL)), _resident((A_QK_W, LANES)),
                  _resident((2 * LANES, A_QK_W)), _resident((1, A_QK_W)), tab, tab],
        out_specs=[row(A_Q_W), row(A_KDUP_W),
                   pl.BlockSpec((1, A_V_W, TM), lambda i: (i // spt, 0, i % spt))],
        out_shape=[jax.ShapeDtypeStruct((n, A_Q_W), BF16),
                   jax.ShapeDtypeStruct((n, A_KDUP_W), BF16),
                   jax.ShapeDtypeStruct((n // seq, A_V_W, seq), BF16)],
        compiler_params=_params("parallel"),
        name="qkv_a",
    )(x, gain, w, wvt, e, et, head_gain, cos, sin)


def _project_out(u, x_ref, ot_scr, wo_ref, o_ref):
    rows = slice(u * TQ, (u + 1) * TQ)
    o = ot_scr[u].T.astype(BF16)
    o_ref[0, rows] = x_ref[0, rows] + _dot(o, wo_ref[...])


def _attn_a_kernel(x_ref, q_ref, k_ref, vt_ref, wo_ref, o_ref, ot_scr, s_scr, p_scr):
    seq = k_ref.shape[1]
    hd = A_HEAD_DIM
    ones = jnp.ones((BF16_ROWS, seq), BF16)
    probs = functools.partial(_probs_t, s_scr, p_scr)

    def scores(i):
        u, h = divmod(i, A_HEADS)
        g = h // A_GROUP
        q_pair = q_ref[0, u * TQ:(u + 1) * TQ, (h // 2) * LANES:(h // 2 + 1) * LANES]
        q_h = jnp.where(_half_lanes_mask(TQ, h % 2 == 1), q_pair, jnp.zeros_like(q_pair))
        part = seq // KEY_SPLITS
        k_halves = [k_ref[0, r * part:(r + 1) * part, g * LANES:(g + 1) * LANES]
                    for r in range(KEY_SPLITS)]
        return _park_scores(s_scr, i, *_scores_t(k_halves, q_h))

    def values(i, _):
        u, h = divmod(i, A_HEADS)
        g = h // A_GROUP
        v_aug = jnp.concatenate([vt_ref[0, g * hd:(g + 1) * hd, :], ones], axis=0)
        ot_scr[u, h * hd:(h + 1) * hd, :] = _weighted_values_t(v_aug, p_scr[i % P_SLOTS])

    project = functools.partial(_project_out, x_ref=x_ref, ot_scr=ot_scr, wo_ref=wo_ref,
                                o_ref=o_ref)
    _pipelined(SUB_TILES * A_HEADS, scores, probs, values,
               _project_late(SUB_TILES, A_HEADS, project))


def _attn_a(x, q, k, vt, wo):
    b, seq, _ = x.shape
    rows = SUB_TILES * TQ
    tile = pl.BlockSpec((1, rows, D_MODEL), lambda i, j: (i, j, 0))
    return pl.pallas_call(
        _attn_a_kernel,
        grid=(b, seq // rows),
        in_specs=[tile, tile,
                  pl.BlockSpec((1, seq, A_KDUP_W), lambda i, j: (i, 0, 0)),
                  pl.BlockSpec((1, A_V_W, seq), lambda i, j: (i, 0, 0)),
                  _resident((D_MODEL, D_MODEL))],
        out_specs=tile,
        out_shape=jax.ShapeDtypeStruct(x.shape, F32),
        scratch_shapes=[pltpu.VMEM((SUB_TILES, D_MODEL, TQ), F32), *_stage_scratch(seq)],
        compiler_params=_params("parallel", "parallel"),
        name="attn_a",
    )(x, q, k, vt, wo)


def _pool_kernel(seq, x_ref, prev_ref, next_ref, g_ref, w_ref, sc_ref, o_ref, h_scr):
    i = pl.program_id(0)
    t0 = (i % (seq // TM)) * TM
    gain = g_ref[...]
    x = x_ref[...]
    h = _rms(x, gain)
    h_scr[0:POOL_HALO, :] = jnp.where(t0 > 0, _rms(prev_ref[...], gain), 0.0)
    h_scr[POOL_HALO:POOL_HALO + TM, :] = h
    h_scr[POOL_HALO + TM:, :] = jnp.where(t0 + TM < seq, _rms(next_ref[...], gain), 0.0)
    t = t0 + lax.broadcasted_iota(jnp.int32, (TM, 1), 0)
    ys = []
    for g, win in enumerate(POOL_WINDOWS):
        lanes = slice(g * POOL_GROUP, (g + 1) * POOL_GROUP)
        half = win // 2
        total = h_scr[POOL_HALO - half:POOL_HALO - half + TM, lanes]
        for d in range(1, win):
            total = total + h_scr[POOL_HALO - half + d:POOL_HALO - half + d + TM, lanes]
        lo = jnp.clip(t - half, 0, seq - 1)
        hi = jnp.clip(t + half - 1, 0, seq - 1)
        cnt = (hi - lo + 1).astype(F32)
        p = (total / cnt - h[:, lanes]).astype(BF16)
        ys.append(_dot(p, w_ref[g]))
    o_ref[...] = x + jnp.concatenate(ys, axis=1) * sc_ref[...]


def _pool(x, gain, w, scale, seq):
    n = x.shape[0]
    halo_per_tile = TM // POOL_HALO
    last_halo = n // POOL_HALO - 1
    row = pl.BlockSpec((TM, D_MODEL), lambda i: (i, 0))
    prev = pl.BlockSpec((POOL_HALO, D_MODEL),
                        lambda i: (jnp.maximum(i * halo_per_tile - 1, 0), 0))
    nxt = pl.BlockSpec((POOL_HALO, D_MODEL),
                       lambda i: (jnp.minimum((i + 1) * halo_per_tile, last_halo), 0))
    ng = len(POOL_WINDOWS)
    return pl.pallas_call(
        functools.partial(_pool_kernel, seq),
        grid=(n // TM,),
        in_specs=[row, prev, nxt, _resident((1, D_MODEL)),
                  _resident((ng, POOL_GROUP, POOL_GROUP)), _resident((1, D_MODEL))],
        out_specs=row,
        out_shape=jax.ShapeDtypeStruct((n, D_MODEL), F32),
        scratch_shapes=[pltpu.VMEM((TM + 2 * POOL_HALO, D_MODEL), F32)],
        compiler_params=_params("parallel"),
        name="pool",
    )(x, x, x, gain, w, scale)


def _qkv_c_kernel(x_ref, g_ref, w_ref, wvt_ref, q_ref, k_ref, vt_ref):
    h = _rms(x_ref[...], g_ref[...]).astype(BF16)
    width = 2 * C_HEADS * C_HEAD_DIM
    qk = _dot(h, w_ref[...])
    q_ref[...] = (qk[:, :width] * (C_HEAD_DIM ** -0.5 * LOG2E)).astype(BF16)
    k_ref[...] = qk[:, width:].astype(BF16)
    vt_ref[0] = _dot_nt(wvt_ref[...], h).astype(BF16)


def _qkv_c(x, gain, w, wvt, seq):
    n = x.shape[0]
    spt = seq // TM
    row = pl.BlockSpec((TM, D_MODEL), lambda i: (i, 0))
    out = jax.ShapeDtypeStruct((n, D_MODEL), BF16)
    return pl.pallas_call(
        _qkv_c_kernel,
        grid=(n // TM,),
        in_specs=[row, _resident((1, D_MODEL)), _resident((D_MODEL, 2 * D_MODEL)),
                  _resident((D_MODEL, D_MODEL))],
        out_specs=[row, row, pl.BlockSpec((1, D_MODEL, TM), lambda i: (i // spt, 0, i % spt))],
        out_shape=[out, out, jax.ShapeDtypeStruct((n // seq, D_MODEL, seq), BF16)],
        compiler_params=_params("parallel"),
        name="qkv_c",
    )(x, gain, w, wvt)


def _lane_pattern(lane, values):
    out = jnp.zeros(lane.shape, F32)
    for l, v in enumerate(values):
        out = jnp.where(lane == l, v, out)
    return out


def _attn_c_kernel(lam_init, nb, x_ref, q_ref, *refs):
    k_refs, vt_refs = refs[:nb], refs[nb:2 * nb]
    lp_ref, sg_ref, wo_ref, o_ref, ot_scr, s_scr, p_scr = refs[2 * nb:]
    probs = functools.partial(_probs_t, s_scr, p_scr)
    hw = 2 * C_HEAD_DIM
    first = pl.program_id(1) * SUB_TILES
    lp = lp_ref[...]
    lam = (jnp.exp(jnp.sum(lp[0:1] * lp[1:2], axis=-1, keepdims=True))
           - jnp.exp(jnp.sum(lp[2:3] * lp[3:4], axis=-1, keepdims=True)) + lam_init)

    lane = lax.broadcasted_iota(jnp.int32, (TQ, LANES), 1)
    offset = lax.broadcasted_iota(jnp.int32, (TQ, LANES), 0).astype(F32)
    e_parts = list(ALIBI_E_PARTS)
    k_aug = [[None] * nb for _ in range(SUB_TILES)]
    for d in range(nb):
        j0 = (((first + d) % nb) * TQ).astype(F32)
        pattern = _lane_pattern(lane, 2 * e_parts + 3 * [j0] + 3 * [offset])
        wrapped = first + d >= nb
        for u in range(SUB_TILES):
            if d == u:
                k_aug[u][d] = jnp.zeros((TQ, LANES), BF16)
            else:
                sigma = 1.0 if d < u else jnp.where(wrapped, 1.0, -1.0)
                k_aug[u][d] = (sigma * pattern).astype(BF16)
    q_aug = []
    for u in range(SUB_TILES):
        i0 = ((first + u) * TQ).astype(F32)
        q_aug0 = _lane_pattern(lane, 3 * [-offset] + 3 * [-i0] + 2 * e_parts)
        q_aug.append([(q_aug0 * 2.0 ** -h).astype(BF16) for h in range(C_HEADS)])
    own_dist = jnp.abs(lax.broadcasted_iota(jnp.int32, (TQ, TQ), 0)
                       - lax.broadcasted_iota(jnp.int32, (TQ, TQ), 1)).astype(F32)
    ones = jnp.ones((BF16_ROWS, nb * TQ), BF16)
    per_half = nb // KEY_SPLITS
    assert SUB_TILES <= per_half

    def scores(i):
        u, rest = divmod(i, 2 * C_HEADS)
        h, c = divmod(rest, 2)
        lanes = slice(h * hw, (h + 1) * hw)
        q_pair = q_ref[0, u * TQ:(u + 1) * TQ, lanes]
        q_c = jnp.where(_half_lanes_mask(TQ, c == 1), q_pair, jnp.zeros_like(q_pair))
        q_full = jnp.concatenate([q_c, q_aug[u][h]], axis=1)
        k_halves = [
            jnp.concatenate(
                [jnp.concatenate([k_refs[d][0, :, lanes], k_aug[u][d]], axis=1)
                 for d in range(r * per_half, (r + 1) * per_half)], axis=0)
            for r in range(KEY_SPLITS)]
        own_bias = own_dist * (ALIBI_E * 2.0 ** -h)
        return _park_scores(s_scr, i, *_scores_t(k_halves, q_full, own_bias, u * TQ))

    def values(i, _):
        u, rest = divmod(i, 2 * C_HEADS)
        h, c = divmod(rest, 2)
        v_t = jnp.concatenate([vt_refs[d][0, h * hw:(h + 1) * hw, :] for d in range(nb)], axis=1)
        v_aug = jnp.concatenate([v_t, ones], axis=0)
        o_c = _weighted_values_t(v_aug, p_scr[i % P_SLOTS])
        rows = slice(h * hw, (h + 1) * hw)
        if c == 0:
            ot_scr[u, rows, :] = o_c
        else:
            o = ot_scr[u, rows, :] - lam * o_c
            ms = jnp.mean(o * o, axis=0, keepdims=True)
            ot_scr[u, rows, :] = o * lax.rsqrt(ms + NORM_EPS) * sg_ref[...] * (1.0 - lam_init)

    project = functools.partial(_project_out, x_ref=x_ref, ot_scr=ot_scr, wo_ref=wo_ref,
                                o_ref=o_ref)
    _pipelined(SUB_TILES * 2 * C_HEADS, scores, probs, values,
               _project_late(SUB_TILES, 2 * C_HEADS, project))


def _attn_c(x, q, k, vt, lam_params, sub_gain_t, wo, lam_init):
    b, seq, _ = x.shape
    nb = seq // TQ
    rows = SUB_TILES * TQ
    tile = pl.BlockSpec((1, rows, D_MODEL), lambda i, j: (i, j, 0))
    k_blocks = [pl.BlockSpec((1, TQ, D_MODEL),
                             lambda i, j, d=d: (i, (j * SUB_TILES + d) % nb, 0))
                for d in range(nb)]
    vt_blocks = [pl.BlockSpec((1, D_MODEL, TQ),
                              lambda i, j, d=d: (i, 0, (j * SUB_TILES + d) % nb))
                 for d in range(nb)]
    return pl.pallas_call(
        functools.partial(_attn_c_kernel, lam_init, nb),
        grid=(b, seq // rows),
        in_specs=[tile, tile, *k_blocks, *vt_blocks,
                  _resident((4, C_HEAD_DIM)), _resident((2 * C_HEAD_DIM, TQ)),
                  _resident((D_MODEL, D_MODEL))],
        out_specs=tile,
        out_shape=jax.ShapeDtypeStruct(x.shape, F32),
        scratch_shapes=[pltpu.VMEM((SUB_TILES, D_MODEL, TQ), F32), *_stage_scratch(seq)],
        compiler_params=_params("parallel", "parallel"),
        name="attn_c",
    )(x, q, *([k] * nb), *([vt] * nb), lam_params, sub_gain_t, wo)


F_EXTRA = BF16_ROWS


def _fourier_kernel(xd_ref, xm_ref, h_ref, ct_ref, st_ref, ctx_ref, stx_ref, cc_ref, sc_ref,
                    rev_ref, wo_ref, o_ref):
    h = h_ref[0]
    ct = jnp.concatenate([ct_ref[...], ctx_ref[...]], axis=0)
    st = jnp.concatenate([st_ref[...], stx_ref[...]], axis=0)
    p = _dot(ct, h).astype(BF16)
    q = _dot(st, h).astype(BF16)
    direct, mirror = [], []
    for g in range(F_GROUPS):
        lanes = slice(g * F_GROUP_DIM, (g + 1) * F_GROUP_DIM)
        a = _dot(p[:, lanes], cc_ref[...])
        b = _dot(q[:, lanes], sc_ref[...])
        direct.append(a[:TM] - b[:TM])
        mirror.append(a + b)
    f = jnp.concatenate(direct, axis=1).astype(BF16)
    o_ref[0, 0, 0] = xd_ref[0, 0] + _dot(f, wo_ref[...])
    fm = jnp.concatenate(mirror, axis=1).astype(BF16)
    rev = _dot(rev_ref[...], fm[:TM])
    row = lax.broadcasted_iota(jnp.int32, (TM, D_MODEL), 0)
    fm_rev = jnp.where(row == 0, fm[TM:TM + 1].astype(F32), rev).astype(BF16)
    o_ref[0, 0, 1] = xm_ref[0, 0] + _dot(fm_rev, wo_ref[...])


def _fourier(x, h, ct, st, cc, sc, rev, wo):
    b, seq, _ = x.shape
    nt = seq // TM
    assert nt == 4
    x4 = x.reshape(b, nt, TM, D_MODEL)
    tile = lambda f: pl.BlockSpec((1, 1, TM, D_MODEL), lambda i, j: (i, f(j), 0, 0))
    whole = pl.BlockSpec((1, seq, D_MODEL), lambda i, j: (i, 0, 0))
    tab = pl.BlockSpec((TM, seq), lambda i, j: (j, 0))
    tab_extra = pl.BlockSpec((F_EXTRA, seq), lambda i, j: ((j + 1) * (TM // F_EXTRA), 0))
    square = _resident((F_GROUP_DIM, F_GROUP_DIM))
    return pl.pallas_call(
        _fourier_kernel,
        grid=(b, nt // 2),
        in_specs=[tile(lambda j: j), tile(lambda j: nt - 1 - j), whole, tab, tab, tab_extra,
                  tab_extra, square, square, _resident((TM, TM)), _resident((D_MODEL, D_MODEL))],
        out_specs=pl.BlockSpec((1, 1, 2, TM, D_MODEL), lambda i, j: (i, j, 0, 0, 0)),
        out_shape=jax.ShapeDtypeStruct((b, nt // 2, 2, TM, D_MODEL), F32),
        compiler_params=_params("parallel", "parallel"),
        name="fourier",
    )(x4, x4, h, ct, st, ct, st, cc, sc, rev, wo)


def _fourier_slot(t):
    return jnp.where(t < 2, 2 * t, 7 - 2 * t)


def _rope_tables(seq):
    rows = seq // GRID_W
    r, c = jnp.meshgrid(jnp.arange(rows), jnp.arange(GRID_W), indexing="ij")
    r = r.reshape(-1).astype(F32)
    c = c.reshape(-1).astype(F32)
    n = A_HEAD_DIM // 4
    freqs = ROPE_THETA ** (-jnp.arange(n, dtype=F32) / n)
    ang = jnp.concatenate([r[:, None] * freqs, c[:, None] * freqs], axis=-1)
    cos = jnp.repeat(jnp.cos(ang), 2, axis=-1)
    sin = jnp.repeat(jnp.sin(ang), 2, axis=-1)
    sign = jnp.where(jnp.arange(A_HEAD_DIM) % 2 == 0, -1.0, 1.0).astype(F32)
    sin = sin * sign
    reps = LANES // A_HEAD_DIM
    return jnp.tile(cos, (1, reps)), jnp.tile(sin, (1, reps))


def _dft_tables(n, scale, rows=None):
    k = jnp.arange(rows or n, dtype=jnp.int32)
    kn = (k[:, None] * jnp.arange(n, dtype=jnp.int32)[None, :]) % n
    ang = kn.astype(F32) * (2.0 * math.pi / n)
    return (jnp.cos(ang) * scale).astype(BF16), (jnp.sin(ang) * scale).astype(BF16)


def _reversal_matrix():
    r = jnp.arange(TM)
    return (r[:, None] + r[None, :] == TM).astype(BF16)


def _head_reduce_tables():
    lane_head = jnp.arange(A_QK_W) // A_HEAD_DIM
    onehot = (lane_head[:, None] == jnp.arange(LANES)[None, :]).astype(F32)
    expand = onehot.T.astype(BF16)
    return (onehot / A_HEAD_DIM).astype(BF16), jnp.concatenate([expand, expand], axis=0)


def _chunk_cols(w):
    return w.astype(BF16).reshape(D_MODEL, D_FF // FF_CHUNK, FF_CHUNK).transpose(1, 0, 2)


def _trunk(x, p):
    b, seq, _ = x.shape
    n = b * seq
    flat = lambda a: a.reshape(n, a.shape[-1])
    per_seq = lambda a: a.reshape(b, seq, a.shape[-1])
    x = flat(x)
    for i in LAYERS:
        kind = i % 4
        gain = p["attn_norm"][i][None]
        ffn_args = (p["ffn_norm"][i][None], p["w_gate"][i], p["w_up"][i], p["w_down"][i])
        if kind == 0:
            q, k, vt = _qkv_a(x, gain, p["a_w_qk"], p["a_w_vt"], p["e"], p["et"],
                              p["a_head_gain"], p["cos"], p["sin"], seq)
            x = flat(_attn_a(per_seq(x), per_seq(q), per_seq(k), vt, p["a_w_o"]))
            x = _ffn(x, *ffn_args, p["final_norm"], "plain")
        elif kind == 1:
            x = _pool(x, gain, p["b_w_pool"], p["b_scale"], seq)
            x = _ffn(x, *ffn_args, p["final_norm"], "plain")
        elif kind == 2:
            lam_init = 0.8 - 0.6 * math.exp(-0.3 * i)
            q, k, vt = _qkv_c(x, gain, p["c_w_qk"], p["c_w_vt"], seq)
            x = flat(_attn_c(per_seq(x), per_seq(q), per_seq(k), vt, p["c_lambda"],
                             p["c_sub_gain_t"], p["c_w_o"], lam_init))
            x, h = _ffn(x, *ffn_args, p["attn_norm"][i + 1][None], "emit")
        else:
            assert i == DEPTH - 1
            x = _fourier(per_seq(x), per_seq(h), p["ct"], p["st"], p["cc"], p["sc"],
                         p["rev"], p["d_w_o"]).reshape(n, D_MODEL)
            tiles = seq // TM
            x = _ffn(x, *ffn_args, p["final_norm"], "final",
                     in_tile=lambda t: (t // tiles) * tiles + _fourier_slot(t % tiles))
    return per_seq(x)


def kernel(x_prompt, x_sample, attn_norm, ffn_norm, final_norm, a_w_qkv, a_q_gain, a_k_gain,
           a_w_o, b_w_pool, b_scale, c_w_qkv, c_lambda, c_sub_gain, c_w_o, d_w_o,
           w_gate, w_up, w_down):
    seq = x_prompt.shape[1]
    assert x_sample.shape[1] == seq and seq % TM == 0 and seq % (SUB_TILES * TQ) == 0
    e, et = _head_reduce_tables()
    cos, sin = _rope_tables(seq)
    ct, st = _dft_tables(seq, seq ** -0.5, rows=seq // 2 + F_EXTRA)
    cc, sc = _dft_tables(F_GROUP_DIM, F_GROUP_DIM ** -0.5)
    wa = a_w_qkv[0]
    a_w_qk = wa[:, :A_QK_W]
    head_gain = jnp.concatenate([jnp.tile(a_q_gain[0], A_HEADS) * (A_HEAD_DIM ** -0.5 * LOG2E),
                                 jnp.tile(a_k_gain[0], A_KV_HEADS)])[None]
    wc = c_w_qkv[0]
    nc = D_FF // FF_CHUNK
    p = dict(
        attn_norm=attn_norm, ffn_norm=ffn_norm, final_norm=final_norm[None],
        a_w_qk=a_w_qk.astype(BF16), a_w_vt=wa[:, A_Q_W + A_V_W:].T.astype(BF16),
        a_head_gain=head_gain, a_w_o=a_w_o[0].astype(BF16),
        e=e, et=et, cos=cos, sin=sin,
        b_w_pool=b_w_pool[0].astype(BF16), b_scale=b_scale[0][None],
        c_w_qk=wc[:, :2 * D_MODEL].astype(BF16), c_w_vt=wc[:, 2 * D_MODEL:].T.astype(BF16),
        c_lambda=c_lambda[0],
        c_sub_gain_t=jnp.broadcast_to(c_sub_gain[0][:, None], (2 * C_HEAD_DIM, TQ)),
        c_w_o=c_w_o[0].astype(BF16),
        d_w_o=d_w_o[0].astype(BF16), ct=ct, st=st, cc=cc, sc=sc, rev=_reversal_matrix(),
        w_gate=[_chunk_cols(w_gate[i]) for i in range(DEPTH)],
        w_up=[_chunk_cols(w_up[i]) for i in range(DEPTH)],
        w_down=[w_down[i].astype(BF16).reshape(nc, FF_CHUNK, D_MODEL) for i in range(DEPTH)],
    )
    return _trunk(x_prompt, p), _trunk(x_sample, p)
```

```python
import functools
import math

import jax
import jax.numpy as jnp
import ml_dtypes
from jax import lax
from jax.experimental import pallas as pl
from jax.experimental.pallas import tpu as pltpu

F32 = jnp.float32
BF16 = jnp.bfloat16

D_MODEL = 1024
DEPTH = 4
LAYERS = tuple(range(DEPTH))
GRID_W = 64
NORM_EPS = 1e-6
A_HEADS = 16
A_KV_HEADS = 4
A_HEAD_DIM = 64
A_GROUP = A_HEADS // A_KV_HEADS
ROPE_THETA = 10000.0
POOL_WINDOWS = (2, 4, 8, 16)
POOL_GROUP = D_MODEL // len(POOL_WINDOWS)
POOL_HALO = 8
C_HEADS = 8
C_HEAD_DIM = 64
F_GROUPS = 4
F_GROUP_DIM = D_MODEL // F_GROUPS
D_FF = 2816
LOG2E = math.log2(math.e)


def _bf16_parts(x, n):
    parts = []
    for _ in range(n):
        parts.append(float(ml_dtypes.bfloat16(x)))
        x -= parts[-1]
    return tuple(parts)


assert C_HEADS == 8
ALIBI_E = 2.0 ** (-8.0 / C_HEADS) * LOG2E
ALIBI_E_PARTS = _bf16_parts(ALIBI_E, 3)

LANES = 128
BF16_ROWS = 16
VMEM_LIMIT = 56 * 1024 * 1024

A_Q_W = A_HEADS * A_HEAD_DIM
A_V_W = A_KV_HEADS * A_HEAD_DIM
A_QK_W = A_Q_W + A_V_W
A_KDUP_W = A_KV_HEADS * LANES

TM = 512
TQ = 256
SUB_TILES = 2
OUT_DELAY = 2
FF_CHUNK = 256
KEY_SPLITS = 2


def _params(*sem):
    return pltpu.CompilerParams(dimension_semantics=sem, vmem_limit_bytes=VMEM_LIMIT)


def _resident(shape):
    nd = len(shape)
    return pl.BlockSpec(shape, lambda *_: (0,) * nd, pipeline_mode=pl.Buffered(1))


def _rms(x, gain):
    ms = jnp.mean(x * x, axis=-1, keepdims=True)
    return x * lax.rsqrt(ms + NORM_EPS) * gain


def _split_bf16(x):
    hi = x.astype(BF16)
    lo = (x - hi.astype(F32)).astype(BF16)
    return hi, lo


def _dot(a, b):
    return jnp.dot(a, b, preferred_element_type=F32)


def _dot_nt(a, b):
    return lax.dot_general(a, b, (((1,), (1,)), ((), ())), preferred_element_type=F32)


def _half_lanes_mask(rows, upper):
    lane = lax.broadcasted_iota(jnp.int32, (rows, LANES), 1)
    return lane >= LANES // 2 if upper else lane < LANES // 2


def _scores_t(k_halves, q, own_bias=None, own_row=0):
    halves = [_dot_nt(k, q) for k in k_halves]
    if own_bias is not None:
        first, end = halves[0], own_row + own_bias.shape[0]
        pieces = [first[:own_row], first[own_row:end] - own_bias, first[end:]]
        halves[0] = jnp.concatenate([piece for piece in pieces if piece.shape[0]], axis=0)
    m = functools.reduce(jnp.maximum, [jnp.max(s_t, axis=0, keepdims=True) for s_t in halves])
    return halves, m


S_SLOTS = 3
P_SLOTS = 2


def _stage_scratch(keys):
    return [pltpu.VMEM((S_SLOTS, keys, TQ), F32), pltpu.VMEM((P_SLOTS, keys, TQ), BF16)]


def _park_scores(s_scr, i, halves, m):
    part = halves[0].shape[0]
    for r, s_t in enumerate(halves):
        s_scr[i % s_scr.shape[0], r * part:(r + 1) * part, :] = s_t
    return i, m


def _probs_t(s_scr, p_scr, i, m):
    p_scr[i % p_scr.shape[0]] = jnp.exp2(s_scr[i % s_scr.shape[0]] - m).astype(BF16)
    return i


def _weighted_values_t(v_aug, p_t):
    d = v_aug.shape[0] - BF16_ROWS
    oa = _dot(v_aug, p_t)
    return oa[:d] / oa[d:d + 1]


def _pipelined(n, scores, probs, values, after=None):
    s = {0: scores(0)}
    if n > 1:
        s[1] = scores(1)
    p = {0: probs(*s.pop(0))}
    for i in range(n):
        if i + 2 < n:
            s[i + 2] = scores(i + 2)
        if i + 1 < n:
            p[i + 1] = probs(*s.pop(i + 1))
        values(i, p.pop(i))
        if after is not None:
            after(i)


def _project_late(n_sub, items_per_sub, project):
    done = []

    def after(i):
        for u in range(n_sub):
            ready = (u + 1) * items_per_sub - 1 + OUT_DELAY
            if u not in done and i >= min(ready, n_sub * items_per_sub - 1):
                done.append(u)
                project(u)
    return after


def _ffn_apply(x, g_ref, wg_ref, wu_ref, wd_ref):
    h = _rms(x, g_ref[...]).astype(BF16)
    acc = x
    for c in range(D_FF // FF_CHUNK):
        cols = slice(c * FF_CHUNK, (c + 1) * FF_CHUNK)
        g = _dot(h, wg_ref[:, cols])
        u = _dot(h, wu_ref[:, cols])
        a = (g * jax.nn.sigmoid(g) * u).astype(BF16)
        acc = acc + _dot(a, wd_ref[cols, :])
    return acc


def _ffn_specs():
    return [_resident((1, D_MODEL)), _resident((D_MODEL, D_FF)), _resident((D_MODEL, D_FF)),
            _resident((D_FF, D_MODEL))]


def _ffn_kernel(mode, x_ref, g_ref, wg_ref, wu_ref, wd_ref, ng_ref, o_ref, *h_ref):
    acc = _ffn_apply(x_ref[...], g_ref, wg_ref, wu_ref, wd_ref)
    if mode == "plain":
        o_ref[...] = acc
        return
    normed = _rms(acc, ng_ref[...])
    o_ref[...] = normed if mode == "final" else acc
    if mode == "emit":
        h_ref[0][...] = normed.astype(BF16)


def _ffn(x, gain, wg, wu, wd, next_gain, mode, in_tile=lambda i: i):
    n = x.shape[0]
    row = pl.BlockSpec((TM, D_MODEL), lambda i: (i, 0))
    out = jax.ShapeDtypeStruct((n, D_MODEL), F32)
    emit = mode == "emit"
    return pl.pallas_call(
        functools.partial(_ffn_kernel, mode),
        grid=(n // TM,),
        in_specs=[pl.BlockSpec((TM, D_MODEL), lambda i: (in_tile(i), 0)), *_ffn_specs(),
                  _resident((1, D_MODEL))],
        out_specs=[row, row] if emit else row,
        out_shape=[out, jax.ShapeDtypeStruct((n, D_MODEL), BF16)] if emit else out,
        compiler_params=_params("parallel"),
        name="ffn_" + mode,
    )(x, gain, wg, wu, wd, next_gain)


def _qkv_a_kernel(x_ref, g_ref, w_ref, wvt_ref, e_ref, et_ref, hg_ref, cos_ref, sin_ref,
                  q_ref, k_ref, vt_ref):
    h = _rms(x_ref[...], g_ref[...]).astype(BF16)
    qk = _dot(h, w_ref[...])
    sq_hi, sq_lo = _split_bf16(qk * qk)
    ms = _dot(sq_hi, e_ref[...]) + _dot(sq_lo, e_ref[...])
    rs = _dot(jnp.concatenate(_split_bf16(lax.rsqrt(ms + NORM_EPS)), axis=1), et_ref[...])
    qn = qk * rs * hg_ref[...]
    lane = lax.broadcasted_iota(jnp.int32, qn.shape, 1)
    partner = jnp.where(lane % 2 == 0, pltpu.roll(qn, A_QK_W - 1, 1), pltpu.roll(qn, 1, 1))
    reps = A_QK_W // LANES
    cos = jnp.concatenate([cos_ref[...]] * reps, axis=1)
    sin = jnp.concatenate([sin_ref[...]] * reps, axis=1)
    out = qn * cos + partner * sin
    q_ref[...] = out[:, :A_Q_W].astype(BF16)
    kk = out[:, A_Q_W:]
    up = pltpu.roll(kk, A_HEAD_DIM, 1)
    down = pltpu.roll(kk, A_V_W - A_HEAD_DIM, 1)
    lower = _half_lanes_mask(kk.shape[0], False)
    for g in range(A_KV_HEADS):
        src = slice((g // 2) * LANES, (g // 2 + 1) * LANES)
        pair = (jnp.where(lower, kk[:, src], up[:, src]) if g % 2 == 0
                else jnp.where(lower, down[:, src], kk[:, src]))
        k_ref[:, g * LANES:(g + 1) * LANES] = pair.astype(BF16)
    vt_ref[0] = _dot_nt(wvt_ref[...], h).astype(BF16)


def _qkv_a(x, gain, w, wvt, e, et, head_gain, cos, sin, seq):
    n = x.shape[0]
    spt = seq // TM
    row = lambda width: pl.BlockSpec((TM, width), lambda i: (i, 0))
    tab = pl.BlockSpec((TM, LANES), lambda i: (i % spt, 0))
    return pl.pallas_call(
        _qkv_a_kernel,
        grid=(n // TM,),
        in_specs=[row(D_MODEL), _resident((1, D_MODEL)), _resident((D_MODEL, A_QK_W)),
                  _resident((A_V_W, D_MODEL)), _resident((A_QK_W, LANES)),
                  _resident((2 * LANES, A_QK_W)), _resident((1, A_QK_W)), tab, tab],
        out_specs=[row(A_Q_W), row(A_KDUP_W),
                   pl.BlockSpec((1, A_V_W, TM), lambda i: (i // spt, 0, i % spt))],
        out_shape=[jax.ShapeDtypeStruct((n, A_Q_W), BF16),
                   jax.ShapeDtypeStruct((n, A_KDUP_W), BF16),
                   jax.ShapeDtypeStruct((n // seq, A_V_W, seq), BF16)],
        compiler_params=_params("parallel"),
        name="qkv_a",
    )(x, gain, w, wvt, e, et, head_gain, cos, sin)


def _project_out(u, x_ref, ot_scr, wo_ref, o_ref):
    rows = slice(u * TQ, (u + 1) * TQ)
    o = ot_scr[u].T.astype(BF16)
    o_ref[0, rows] = x_ref[0, rows] + _dot(o, wo_ref[...])


def _attn_a_kernel(x_ref, q_ref, k_ref, vt_ref, wo_ref, o_ref, ot_scr, s_scr, p_scr):
    seq = k_ref.shape[1]
    hd = A_HEAD_DIM
    ones = jnp.ones((BF16_ROWS, seq), BF16)
    probs = functools.partial(_probs_t, s_scr, p_scr)

    def scores(i):
        u, h = divmod(i, A_HEADS)
        g = h // A_GROUP
        q_pair = q_ref[0, u * TQ:(u + 1) * TQ, (h // 2) * LANES:(h // 2 + 1) * LANES]
        q_h = jnp.where(_half_lanes_mask(TQ, h % 2 == 1), q_pair, jnp.zeros_like(q_pair))
        part = seq // KEY_SPLITS
        k_halves = [k_ref[0, r * part:(r + 1) * part, g * LANES:(g + 1) * LANES]
                    for r in range(KEY_SPLITS)]
        return _park_scores(s_scr, i, *_scores_t(k_halves, q_h))

    def values(i, _):
        u, h = divmod(i, A_HEADS)
        g = h // A_GROUP
        v_aug = jnp.concatenate([vt_ref[0, g * hd:(g + 1) * hd, :], ones], axis=0)
        ot_scr[u, h * hd:(h + 1) * hd, :] = _weighted_values_t(v_aug, p_scr[i % P_SLOTS])

    project = functools.partial(_project_out, x_ref=x_ref, ot_scr=ot_scr, wo_ref=wo_ref,
                                o_ref=o_ref)
    _pipelined(SUB_TILES * A_HEADS, scores, probs, values,
               _project_late(SUB_TILES, A_HEADS, project))


def _attn_a(x, q, k, vt, wo):
    b, seq, _ = x.shape
    rows = SUB_TILES * TQ
    tile = pl.BlockSpec((1, rows, D_MODEL), lambda i, j: (i, j, 0))
    return pl.pallas_call(
        _attn_a_kernel,
        grid=(b, seq // rows),
        in_specs=[tile, tile,
                  pl.BlockSpec((1, seq, A_KDUP_W), lambda i, j: (i, 0, 0)),
                  pl.BlockSpec((1, A_V_W, seq), lambda i, j: (i, 0, 0)),
                  _resident((D_MODEL, D_MODEL))],
        out_specs=tile,
        out_shape=jax.ShapeDtypeStruct(x.shape, F32),
        scratch_shapes=[pltpu.VMEM((SUB_TILES, D_MODEL, TQ), F32), *_stage_scratch(seq)],
        compiler_params=_params("parallel", "parallel"),
        name="attn_a",
    )(x, q, k, vt, wo)


def _pool_kernel(seq, x_ref, prev_ref, next_ref, g_ref, w_ref, sc_ref, o_ref, h_scr):
    i = pl.program_id(0)
    t0 = (i % (seq // TM)) * TM
    gain = g_ref[...]
    x = x_ref[...]
    h = _rms(x, gain)
    h_scr[0:POOL_HALO, :] = jnp.where(t0 > 0, _rms(prev_ref[...], gain), 0.0)
    h_scr[POOL_HALO:POOL_HALO + TM, :] = h
    h_scr[POOL_HALO + TM:, :] = jnp.where(t0 + TM < seq, _rms(next_ref[...], gain), 0.0)
    t = t0 + lax.broadcasted_iota(jnp.int32, (TM, 1), 0)
    ys = []
    for g, win in enumerate(POOL_WINDOWS):
        lanes = slice(g * POOL_GROUP, (g + 1) * POOL_GROUP)
        half = win // 2
        total = h_scr[POOL_HALO - half:POOL_HALO - half + TM, lanes]
        for d in range(1, win):
            total = total + h_scr[POOL_HALO - half + d:POOL_HALO - half + d + TM, lanes]
        lo = jnp.clip(t - half, 0, seq - 1)
        hi = jnp.clip(t + half - 1, 0, seq - 1)
        cnt = (hi - lo + 1).astype(F32)
        p = (total / cnt - h[:, lanes]).astype(BF16)
        ys.append(_dot(p, w_ref[g]))
    o_ref[...] = x + jnp.concatenate(ys, axis=1) * sc_ref[...]


def _pool(x, gain, w, scale, seq):
    n = x.shape[0]
    halo_per_tile = TM // POOL_HALO
    last_halo = n // POOL_HALO - 1
    row = pl.BlockSpec((TM, D_MODEL), lambda i: (i, 0))
    prev = pl.BlockSpec((POOL_HALO, D_MODEL),
                        lambda i: (jnp.maximum(i * halo_per_tile - 1, 0), 0))
    nxt = pl.BlockSpec((POOL_HALO, D_MODEL),
                       lambda i: (jnp.minimum((i + 1) * halo_per_tile, last_halo), 0))
    ng = len(POOL_WINDOWS)
    return pl.pallas_call(
        functools.partial(_pool_kernel, seq),
        grid=(n // TM,),
        in_specs=[row, prev, nxt, _resident((1, D_MODEL)),
                  _resident((ng, POOL_GROUP, POOL_GROUP)), _resident((1, D_MODEL))],
        out_specs=row,
        out_shape=jax.ShapeDtypeStruct((n, D_MODEL), F32),
        scratch_shapes=[pltpu.VMEM((TM + 2 * POOL_HALO, D_MODEL), F32)],
        compiler_params=_params("parallel"),
        name="pool",
    )(x, x, x, gain, w, scale)


def _qkv_c_kernel(x_ref, g_ref, w_ref, wvt_ref, q_ref, k_ref, vt_ref):
    h = _rms(x_ref[...], g_ref[...]).astype(BF16)
    width = 2 * C_HEADS * C_HEAD_DIM
    qk = _dot(h, w_ref[...])
    q_ref[...] = (qk[:, :width] * (C_HEAD_DIM ** -0.5 * LOG2E)).astype(BF16)
    k_ref[...] = qk[:, width:].astype(BF16)
    vt_ref[0] = _dot_nt(wvt_ref[...], h).astype(BF16)


def _qkv_c(x, gain, w, wvt, seq):
    n = x.shape[0]
    spt = seq // TM
    row = pl.BlockSpec((TM, D_MODEL), lambda i: (i, 0))
    out = jax.ShapeDtypeStruct((n, D_MODEL), BF16)
    return pl.pallas_call(
        _qkv_c_kernel,
        grid=(n // TM,),
        in_specs=[row, _resident((1, D_MODEL)), _resident((D_MODEL, 2 * D_MODEL)),
                  _resident((D_MODEL, D_MODEL))],
        out_specs=[row, row, pl.BlockSpec((1, D_MODEL, TM), lambda i: (i // spt, 0, i % spt))],
        out_shape=[out, out, jax.ShapeDtypeStruct((n // seq, D_MODEL, seq), BF16)],
        compiler_params=_params("parallel"),
        name="qkv_c",
    )(x, gain, w, wvt)


def _lane_pattern(lane, values):
    out = jnp.zeros(lane.shape, F32)
    for l, v in enumerate(values):
        out = jnp.where(lane == l, v, out)
    return out


def _attn_c_kernel(lam_init, nb, x_ref, q_ref, *refs):
    k_refs, vt_refs = refs[:nb], refs[nb:2 * nb]
    (lp_ref, sg_ref, wo_ref, o_ref, ot_scr, s_scr, p_scr, kaug_scr, qaug_scr,
     bias_scr) = refs[2 * nb:]
    probs = functools.partial(_probs_t, s_scr, p_scr)
    hw = 2 * C_HEAD_DIM
    first = pl.program_id(1) * SUB_TILES
    lp = lp_ref[...]
    lam = (jnp.exp(jnp.sum(lp[0:1] * lp[1:2], axis=-1, keepdims=True))
           - jnp.exp(jnp.sum(lp[2:3] * lp[3:4], axis=-1, keepdims=True)) + lam_init)

    lane = lax.broadcasted_iota(jnp.int32, (TQ, LANES), 1)
    offset = lax.broadcasted_iota(jnp.int32, (TQ, LANES), 0).astype(F32)
    e_parts = list(ALIBI_E_PARTS)
    for d in range(nb):
        j0 = (((first + d) % nb) * TQ).astype(F32)
        pattern = _lane_pattern(lane, 2 * e_parts + 3 * [j0] + 3 * [offset])
        wrapped = first + d >= nb
        for u in range(SUB_TILES):
            if d == u:
                kaug_scr[u, d] = jnp.zeros((TQ, LANES), BF16)
            else:
                sigma = 1.0 if d < u else jnp.where(wrapped, 1.0, -1.0)
                kaug_scr[u, d] = (sigma * pattern).astype(BF16)
    for u in range(SUB_TILES):
        i0 = ((first + u) * TQ).astype(F32)
        q_aug0 = _lane_pattern(lane, 3 * [-offset] + 3 * [-i0] + 2 * e_parts)
        for h in range(C_HEADS):
            qaug_scr[u, h] = (q_aug0 * 2.0 ** -h).astype(BF16)
    own_dist = jnp.abs(lax.broadcasted_iota(jnp.int32, (TQ, TQ), 0)
                       - lax.broadcasted_iota(jnp.int32, (TQ, TQ), 1)).astype(F32)
    for h in range(C_HEADS):
        bias_scr[h] = own_dist * (ALIBI_E * 2.0 ** -h)
    ones = jnp.ones((BF16_ROWS, nb * TQ), BF16)
    per_half = nb // KEY_SPLITS
    assert SUB_TILES <= per_half

    def scores(i):
        u, rest = divmod(i, 2 * C_HEADS)
        h, c = divmod(rest, 2)
        lanes = slice(h * hw, (h + 1) * hw)
        q_pair = q_ref[0, u * TQ:(u + 1) * TQ, lanes]
        q_c = jnp.where(_half_lanes_mask(TQ, c == 1), q_pair, jnp.zeros_like(q_pair))
        q_full = jnp.concatenate([q_c, qaug_scr[u, h]], axis=1)
        k_halves = [
            jnp.concatenate(
                [jnp.concatenate([k_refs[d][0, :, lanes], kaug_scr[u, d]], axis=1)
                 for d in range(r * per_half, (r + 1) * per_half)], axis=0)
            for r in range(KEY_SPLITS)]
        return _park_scores(s_scr, i, *_scores_t(k_halves, q_full, bias_scr[h], u * TQ))

    def values(i, _):
        u, rest = divmod(i, 2 * C_HEADS)
        h, c = divmod(rest, 2)
        v_t = jnp.concatenate([vt_refs[d][0, h * hw:(h + 1) * hw, :] for d in range(nb)], axis=1)
        v_aug = jnp.concatenate([v_t, ones], axis=0)
        o_c = _weighted_values_t(v_aug, p_scr[i % P_SLOTS])
        rows = slice(h * hw, (h + 1) * hw)
        if c == 0:
            ot_scr[u, rows, :] = o_c
        else:
            o = ot_scr[u, rows, :] - lam * o_c
            ms = jnp.mean(o * o, axis=0, keepdims=True)
            ot_scr[u, rows, :] = o * lax.rsqrt(ms + NORM_EPS) * sg_ref[...] * (1.0 - lam_init)

    project = functools.partial(_project_out, x_ref=x_ref, ot_scr=ot_scr, wo_ref=wo_ref,
                                o_ref=o_ref)
    _pipelined(SUB_TILES * 2 * C_HEADS, scores, probs, values,
               _project_late(SUB_TILES, 2 * C_HEADS, project))


def _attn_c(x, q, k, vt, lam_params, sub_gain_t, wo, lam_init):
    b, seq, _ = x.shape
    nb = seq // TQ
    rows = SUB_TILES * TQ
    tile = pl.BlockSpec((1, rows, D_MODEL), lambda i, j: (i, j, 0))
    k_blocks = [pl.BlockSpec((1, TQ, D_MODEL),
                             lambda i, j, d=d: (i, (j * SUB_TILES + d) % nb, 0))
                for d in range(nb)]
    vt_blocks = [pl.BlockSpec((1, D_MODEL, TQ),
                              lambda i, j, d=d: (i, 0, (j * SUB_TILES + d) % nb))
                 for d in range(nb)]
    return pl.pallas_call(
        functools.partial(_attn_c_kernel, lam_init, nb),
        grid=(b, seq // rows),
        in_specs=[tile, tile, *k_blocks, *vt_blocks,
                  _resident((4, C_HEAD_DIM)), _resident((2 * C_HEAD_DIM, TQ)),
                  _resident((D_MODEL, D_MODEL))],
        out_specs=tile,
        out_shape=jax.ShapeDtypeStruct(x.shape, F32),
        scratch_shapes=[pltpu.VMEM((SUB_TILES, D_MODEL, TQ), F32), *_stage_scratch(seq),
                        pltpu.VMEM((SUB_TILES, nb, TQ, LANES), BF16),
                        pltpu.VMEM((SUB_TILES, C_HEADS, TQ, LANES), BF16),
                        pltpu.VMEM((C_HEADS, TQ, TQ), F32)],
        compiler_params=_params("parallel", "parallel"),
        name="attn_c",
    )(x, q, *([k] * nb), *([vt] * nb), lam_params, sub_gain_t, wo)


F_EXTRA = BF16_ROWS


def _fourier_kernel(xd_ref, xm_ref, h_ref, ct_ref, st_ref, ctx_ref, stx_ref, cc_ref, sc_ref,
                    rev_ref, wo_ref, o_ref):
    h = h_ref[0]
    ct = jnp.concatenate([ct_ref[...], ctx_ref[...]], axis=0)
    st = jnp.concatenate([st_ref[...], stx_ref[...]], axis=0)
    p = _dot(ct, h).astype(BF16)
    q = _dot(st, h).astype(BF16)
    direct, mirror = [], []
    for g in range(F_GROUPS):
        lanes = slice(g * F_GROUP_DIM, (g + 1) * F_GROUP_DIM)
        a = _dot(p[:, lanes], cc_ref[...])
        b = _dot(q[:, lanes], sc_ref[...])
        direct.append(a[:TM] - b[:TM])
        mirror.append(a + b)
    f = jnp.concatenate(direct, axis=1).astype(BF16)
    o_ref[0, 0, 0] = xd_ref[0, 0] + _dot(f, wo_ref[...])
    fm = jnp.concatenate(mirror, axis=1).astype(BF16)
    rev = _dot(rev_ref[...], fm[:TM])
    row = lax.broadcasted_iota(jnp.int32, (TM, D_MODEL), 0)
    fm_rev = jnp.where(row == 0, fm[TM:TM + 1].astype(F32), rev).astype(BF16)
    o_ref[0, 0, 1] = xm_ref[0, 0] + _dot(fm_rev, wo_ref[...])


def _fourier(x, h, ct, st, cc, sc, rev, wo):
    b, seq, _ = x.shape
    nt = seq // TM
    assert nt == 4
    x4 = x.reshape(b, nt, TM, D_MODEL)
    tile = lambda f: pl.BlockSpec((1, 1, TM, D_MODEL), lambda i, j: (i, f(j), 0, 0))
    whole = pl.BlockSpec((1, seq, D_MODEL), lambda i, j: (i, 0, 0))
    tab = pl.BlockSpec((TM, seq), lambda i, j: (j, 0))
    tab_extra = pl.BlockSpec((F_EXTRA, seq), lambda i, j: ((j + 1) * (TM // F_EXTRA), 0))
    square = _resident((F_GROUP_DIM, F_GROUP_DIM))
    return pl.pallas_call(
        _fourier_kernel,
        grid=(b, nt // 2),
        in_specs=[tile(lambda j: j), tile(lambda j: nt - 1 - j), whole, tab, tab, tab_extra,
                  tab_extra, square, square, _resident((TM, TM)), _resident((D_MODEL, D_MODEL))],
        out_specs=pl.BlockSpec((1, 1, 2, TM, D_MODEL), lambda i, j: (i, j, 0, 0, 0)),
        out_shape=jax.ShapeDtypeStruct((b, nt // 2, 2, TM, D_MODEL), F32),
        compiler_params=_params("parallel", "parallel"),
        name="fourier",
    )(x4, x4, h, ct, st, ct, st, cc, sc, rev, wo)


def _fourier_slot(t):
    return jnp.where(t < 2, 2 * t, 7 - 2 * t)


def _rope_tables(seq):
    rows = seq // GRID_W
    r, c = jnp.meshgrid(jnp.arange(rows), jnp.arange(GRID_W), indexing="ij")
    r = r.reshape(-1).astype(F32)
    c = c.reshape(-1).astype(F32)
    n = A_HEAD_DIM // 4
    freqs = ROPE_THETA ** (-jnp.arange(n, dtype=F32) / n)
    ang = jnp.concatenate([r[:, None] * freqs, c[:, None] * freqs], axis=-1)
    cos = jnp.repeat(jnp.cos(ang), 2, axis=-1)
    sin = jnp.repeat(jnp.sin(ang), 2, axis=-1)
    sign = jnp.where(jnp.arange(A_HEAD_DIM) % 2 == 0, -1.0, 1.0).astype(F32)
    sin = sin * sign
    reps = LANES // A_HEAD_DIM
    return jnp.tile(cos, (1, reps)), jnp.tile(sin, (1, reps))


def _dft_tables(n, scale, rows=None):
    k = jnp.arange(rows or n, dtype=jnp.int32)
    kn = (k[:, None] * jnp.arange(n, dtype=jnp.int32)[None, :]) % n
    ang = kn.astype(F32) * (2.0 * math.pi / n)
    return (jnp.cos(ang) * scale).astype(BF16), (jnp.sin(ang) * scale).astype(BF16)


def _reversal_matrix():
    r = jnp.arange(TM)
    return (r[:, None] + r[None, :] == TM).astype(BF16)


def _head_reduce_tables():
    lane_head = jnp.arange(A_QK_W) // A_HEAD_DIM
    onehot = (lane_head[:, None] == jnp.arange(LANES)[None, :]).astype(F32)
    expand = onehot.T.astype(BF16)
    return (onehot / A_HEAD_DIM).astype(BF16), jnp.concatenate([expand, expand], axis=0)


def _trunk(x, p):
    b, seq, _ = x.shape
    n = b * seq
    flat = lambda a: a.reshape(n, a.shape[-1])
    per_seq = lambda a: a.reshape(b, seq, a.shape[-1])
    x = flat(x)
    for i in LAYERS:
        kind = i % 4
        gain = p["attn_norm"][i][None]
        ffn_args = (p["ffn_norm"][i][None], p["w_gate"][i], p["w_up"][i], p["w_down"][i])
        if kind == 0:
            q, k, vt = _qkv_a(x, gain, p["a_w_qk"], p["a_w_vt"], p["e"], p["et"],
                              p["a_head_gain"], p["cos"], p["sin"], seq)
            x = flat(_attn_a(per_seq(x), per_seq(q), per_seq(k), vt, p["a_w_o"]))
            x = _ffn(x, *ffn_args, p["final_norm"], "plain")
        elif kind == 1:
            x = _pool(x, gain, p["b_w_pool"], p["b_scale"], seq)
            x = _ffn(x, *ffn_args, p["final_norm"], "plain")
        elif kind == 2:
            lam_init = 0.8 - 0.6 * math.exp(-0.3 * i)
            q, k, vt = _qkv_c(x, gain, p["c_w_qk"], p["c_w_vt"], seq)
            x = flat(_attn_c(per_seq(x), per_seq(q), per_seq(k), vt, p["c_lambda"],
                             p["c_sub_gain_t"], p["c_w_o"], lam_init))
            x, h = _ffn(x, *ffn_args, p["attn_norm"][i + 1][None], "emit")
        else:
            assert i == DEPTH - 1
            x = _fourier(per_seq(x), per_seq(h), p["ct"], p["st"], p["cc"], p["sc"],
                         p["rev"], p["d_w_o"]).reshape(n, D_MODEL)
            tiles = seq // TM
            x = _ffn(x, *ffn_args, p["final_norm"], "final",
                     in_tile=lambda t: (t // tiles) * tiles + _fourier_slot(t % tiles))
    return per_seq(x)


def kernel(x_prompt, x_sample, attn_norm, ffn_norm, final_norm, a_w_qkv, a_q_gain, a_k_gain,
           a_w_o, b_w_pool, b_scale, c_w_qkv, c_lambda, c_sub_gain, c_w_o, d_w_o,
           w_gate, w_up, w_down):
    seq = x_prompt.shape[1]
    assert x_sample.shape[1] == seq and seq % TM == 0 and seq % (SUB_TILES * TQ) == 0
    e, et = _head_reduce_tables()
    cos, sin = _rope_tables(seq)
    ct, st = _dft_tables(seq, seq ** -0.5, rows=seq // 2 + F_EXTRA)
    cc, sc = _dft_tables(F_GROUP_DIM, F_GROUP_DIM ** -0.5)
    wa = a_w_qkv[0]
    a_w_qk = wa[:, :A_QK_W]
    head_gain = jnp.concatenate([jnp.tile(a_q_gain[0], A_HEADS) * (A_HEAD_DIM ** -0.5 * LOG2E),
                                 jnp.tile(a_k_gain[0], A_KV_HEADS)])[None]
    wc = c_w_qkv[0]
    p = dict(
        attn_norm=attn_norm, ffn_norm=ffn_norm, final_norm=final_norm[None],
        a_w_qk=a_w_qk.astype(BF16), a_w_vt=wa[:, A_Q_W + A_V_W:].T.astype(BF16),
        a_head_gain=head_gain, a_w_o=a_w_o[0].astype(BF16),
        e=e, et=et, cos=cos, sin=sin,
        b_w_pool=b_w_pool[0].astype(BF16), b_scale=b_scale[0][None],
        c_w_qk=wc[:, :2 * D_MODEL].astype(BF16), c_w_vt=wc[:, 2 * D_MODEL:].T.astype(BF16),
        c_lambda=c_lambda[0],
        c_sub_gain_t=jnp.broadcast_to(c_sub_gain[0][:, None], (2 * C_HEAD_DIM, TQ)),
        c_w_o=c_w_o[0].astype(BF16),
        d_w_o=d_w_o[0].astype(BF16), ct=ct, st=st, cc=cc, sc=sc, rev=_reversal_matrix(),
        w_gate=w_gate.astype(BF16), w_up=w_up.astype(BF16), w_down=w_down.astype(BF16),
    )
    return _trunk(x_prompt, p), _trunk(x_sample, p)
```

```python
import functools
import math

import jax
import jax.numpy as jnp
import ml_dtypes
from jax import lax
from jax.experimental import pallas as pl
from jax.experimental.pallas import tpu as pltpu

F32 = jnp.float32
BF16 = jnp.bfloat16

D_MODEL = 1024
DEPTH = 4
LAYERS = tuple(range(DEPTH))
GRID_W = 64
NORM_EPS = 1e-6
A_HEADS = 16
A_KV_HEADS = 4
A_HEAD_DIM = 64
A_GROUP = A_HEADS // A_KV_HEADS
ROPE_THETA = 10000.0
POOL_WINDOWS = (2, 4, 8, 16)
POOL_GROUP = D_MODEL // len(POOL_WINDOWS)
POOL_HALO = 8
C_HEADS = 8
C_HEAD_DIM = 64
F_GROUPS = 4
F_GROUP_DIM = D_MODEL // F_GROUPS
D_FF = 2816
LOG2E = math.log2(math.e)


def _bf16_parts(x, n):
    parts = []
    for _ in range(n):
        parts.append(float(ml_dtypes.bfloat16(x)))
        x -= parts[-1]
    return tuple(parts)


assert C_HEADS == 8
ALIBI_E = 2.0 ** (-8.0 / C_HEADS) * LOG2E
ALIBI_E_PARTS = _bf16_parts(ALIBI_E, 3)

LANES = 128
BF16_ROWS = 16
VMEM_LIMIT = 56 * 1024 * 1024

A_Q_W = A_HEADS * A_HEAD_DIM
A_V_W = A_KV_HEADS * A_HEAD_DIM
A_QK_W = A_Q_W + A_V_W
A_KDUP_W = A_KV_HEADS * LANES

TM = 512
FFN_TM = 2 * TM
TQ = 256
SUB_TILES = 2
OUT_DELAY = 2
FF_CHUNK = 256
KEY_SPLITS = 2


def _params(*sem):
    return pltpu.CompilerParams(dimension_semantics=sem, vmem_limit_bytes=VMEM_LIMIT)


def _resident(shape):
    nd = len(shape)
    return pl.BlockSpec(shape, lambda *_: (0,) * nd, pipeline_mode=pl.Buffered(1))


def _rms(x, gain):
    ms = jnp.mean(x * x, axis=-1, keepdims=True)
    return x * lax.rsqrt(ms + NORM_EPS) * gain


def _split_bf16(x):
    hi = x.astype(BF16)
    lo = (x - hi.astype(F32)).astype(BF16)
    return hi, lo


def _dot(a, b):
    return jnp.dot(a, b, preferred_element_type=F32)


def _dot_nt(a, b):
    return lax.dot_general(a, b, (((1,), (1,)), ((), ())), preferred_element_type=F32)


def _half_lanes_mask(rows, upper):
    lane = lax.broadcasted_iota(jnp.int32, (rows, LANES), 1)
    return lane >= LANES // 2 if upper else lane < LANES // 2


def _scores_t(k_halves, q, own_bias=None, own_row=0):
    halves = [_dot_nt(k, q) for k in k_halves]
    if own_bias is not None:
        first, end = halves[0], own_row + own_bias.shape[0]
        pieces = [first[:own_row], first[own_row:end] - own_bias, first[end:]]
        halves[0] = jnp.concatenate([piece for piece in pieces if piece.shape[0]], axis=0)
    m = functools.reduce(jnp.maximum, [jnp.max(s_t, axis=0, keepdims=True) for s_t in halves])
    return halves, m


S_SLOTS = 3
P_SLOTS = 2


def _stage_scratch(keys):
    return [pltpu.VMEM((S_SLOTS, keys, TQ), F32), pltpu.VMEM((P_SLOTS, keys, TQ), BF16)]


def _park_scores(s_scr, i, halves, m):
    part = halves[0].shape[0]
    for r, s_t in enumerate(halves):
        s_scr[i % s_scr.shape[0], r * part:(r + 1) * part, :] = s_t
    return i, m


def _probs_t(s_scr, p_scr, i, m):
    p_scr[i % p_scr.shape[0]] = jnp.exp2(s_scr[i % s_scr.shape[0]] - m).astype(BF16)
    return i


def _weighted_values_t(v_aug, p_t):
    d = v_aug.shape[0] - BF16_ROWS
    oa = _dot(v_aug, p_t)
    return oa[:d] / oa[d:d + 1]


def _pipelined(n, scores, probs, values, after=None):
    s = {0: scores(0)}
    if n > 1:
        s[1] = scores(1)
    p = {0: probs(*s.pop(0))}
    for i in range(n):
        if i + 2 < n:
            s[i + 2] = scores(i + 2)
        if i + 1 < n:
            p[i + 1] = probs(*s.pop(i + 1))
        values(i, p.pop(i))
        if after is not None:
            after(i)


def _project_late(n_sub, items_per_sub, project):
    done = []

    def after(i):
        for u in range(n_sub):
            ready = (u + 1) * items_per_sub - 1 + OUT_DELAY
            if u not in done and i >= min(ready, n_sub * items_per_sub - 1):
                done.append(u)
                project(u)
    return after


def _ffn_apply(x, g_ref, wg_ref, wu_ref, wd_ref):
    h = _rms(x, g_ref[...]).astype(BF16)
    acc = x
    for c in range(D_FF // FF_CHUNK):
        cols = slice(c * FF_CHUNK, (c + 1) * FF_CHUNK)
        g = _dot(h, wg_ref[:, cols])
        u = _dot(h, wu_ref[:, cols])
        a = (g * jax.nn.sigmoid(g) * u).astype(BF16)
        acc = acc + _dot(a, wd_ref[cols, :])
    return acc


def _ffn_specs():
    return [_resident((1, D_MODEL)), _resident((D_MODEL, D_FF)), _resident((D_MODEL, D_FF)),
            _resident((D_FF, D_MODEL))]


def _ffn_kernel(mode, x_ref, g_ref, wg_ref, wu_ref, wd_ref, ng_ref, o_ref, *h_ref):
    acc = _ffn_apply(x_ref[...], g_ref, wg_ref, wu_ref, wd_ref)
    if mode == "plain":
        o_ref[...] = acc
        return
    normed = _rms(acc, ng_ref[...])
    o_ref[...] = normed if mode == "final" else acc
    if mode == "emit":
        h_ref[0][...] = normed.astype(BF16)


def _ffn(x, gain, wg, wu, wd, next_gain, mode, in_tile=lambda i: i):
    n = x.shape[0]
    tm = TM if mode == "final" else FFN_TM
    row = pl.BlockSpec((tm, D_MODEL), lambda i: (i, 0))
    out = jax.ShapeDtypeStruct((n, D_MODEL), F32)
    emit = mode == "emit"
    return pl.pallas_call(
        functools.partial(_ffn_kernel, mode),
        grid=(n // tm,),
        in_specs=[pl.BlockSpec((tm, D_MODEL), lambda i: (in_tile(i), 0)), *_ffn_specs(),
                  _resident((1, D_MODEL))],
        out_specs=[row, row] if emit else row,
        out_shape=[out, jax.ShapeDtypeStruct((n, D_MODEL), BF16)] if emit else out,
        compiler_params=_params("parallel"),
        name="ffn_" + mode,
    )(x, gain, wg, wu, wd, next_gain)


def _qkv_a_kernel(x_ref, g_ref, w_ref, wvt_ref, e_ref, et_ref, hg_ref, cos_ref, sin_ref,
                  q_ref, k_ref, vt_ref):
    h = _rms(x_ref[...], g_ref[...]).astype(BF16)
    qk = _dot(h, w_ref[...])
    sq_hi, sq_lo = _split_bf16(qk * qk)
    ms = _dot(sq_hi, e_ref[...]) + _dot(sq_lo, e_ref[...])
    rs = _dot(jnp.concatenate(_split_bf16(lax.rsqrt(ms + NORM_EPS)), axis=1), et_ref[...])
    qn = qk * rs * hg_ref[...]
    lane = lax.broadcasted_iota(jnp.int32, qn.shape, 1)
    partner = jnp.where(lane % 2 == 0, pltpu.roll(qn, A_QK_W - 1, 1), pltpu.roll(qn, 1, 1))
    reps = A_QK_W // LANES
    cos = jnp.concatenate([cos_ref[...]] * reps, axis=1)
    sin = jnp.concatenate([sin_ref[...]] * reps, axis=1)
    out = qn * cos + partner * sin
    q_ref[...] = out[:, :A_Q_W].astype(BF16)
    kk = out[:, A_Q_W:]
    up = pltpu.roll(kk, A_HEAD_DIM, 1)
    down = pltpu.roll(kk, A_V_W - A_HEAD_DIM, 1)
    lower = _half_lanes_mask(kk.shape[0], False)
    for g in range(A_KV_HEADS):
        src = slice((g // 2) * LANES, (g // 2 + 1) * LANES)
        pair = (jnp.where(lower, kk[:, src], up[:, src]) if g % 2 == 0
                else jnp.where(lower, down[:, src], kk[:, src]))
        k_ref[:, g * LANES:(g + 1) * LANES] = pair.astype(BF16)
    vt_ref[0] = _dot_nt(wvt_ref[...], h).astype(BF16)


def _qkv_a(x, gain, w, wvt, e, et, head_gain, cos, sin, seq):
    n = x.shape[0]
    spt = seq // TM
    row = lambda width: pl.BlockSpec((TM, width), lambda i: (i, 0))
    tab = pl.BlockSpec((TM, LANES), lambda i: (i % spt, 0))
    return pl.pallas_call(
        _qkv_a_kernel,
        grid=(n // TM,),
        in_specs=[row(D_MODEL), _resident((1, D_MODEL)), _resident((D_MODEL, A_QK_W)),
                  _resident((A_V_W, D_MODEL)), _resident((A_QK_W, LANES)),
                  _resident((2 * LANES, A_QK_W)), _resident((1, A_QK_W)), tab, tab],
        out_specs=[row(A_Q_W), row(A_KDUP_W),
                   pl.BlockSpec((1, A_V_W, TM), lambda i: (i // spt, 0, i % spt))],
        out_shape=[jax.ShapeDtypeStruct((n, A_Q_W), BF16),
                   jax.ShapeDtypeStruct((n, A_KDUP_W), BF16),
                   jax.ShapeDtypeStruct((n // seq, A_V_W, seq), BF16)],
        compiler_params=_params("parallel"),
        name="qkv_a",
    )(x, gain, w, wvt, e, et, head_gain, cos, sin)


def _project_out(u, x_ref, ot_scr, wo_ref, o_ref):
    rows = slice(u * TQ, (u + 1) * TQ)
    o = ot_scr[u].T.astype(BF16)
    o_ref[0, rows] = x_ref[0, rows] + _dot(o, wo_ref[...])


def _attn_a_kernel(x_ref, q_ref, k_ref, vt_ref, wo_ref, o_ref, ot_scr, s_scr, p_scr):
    seq = k_ref.shape[1]
    hd = A_HEAD_DIM
    ones = jnp.ones((BF16_ROWS, seq), BF16)
    probs = functools.partial(_probs_t, s_scr, p_scr)

    def scores(i):
        u, h = divmod(i, A_HEADS)
        g = h // A_GROUP
        q_pair = q_ref[0, u * TQ:(u + 1) * TQ, (h // 2) * LANES:(h // 2 + 1) * LANES]
        q_h = jnp.where(_half_lanes_mask(TQ, h % 2 == 1), q_pair, jnp.zeros_like(q_pair))
        part = seq // KEY_SPLITS
        k_halves = [k_ref[0, r * part:(r + 1) * part, g * LANES:(g + 1) * LANES]
                    for r in range(KEY_SPLITS)]
        return _park_scores(s_scr, i, *_scores_t(k_halves, q_h))

    def values(i, _):
        u, h = divmod(i, A_HEADS)
        g = h // A_GROUP
        v_aug = jnp.concatenate([vt_ref[0, g * hd:(g + 1) * hd, :], ones], axis=0)
        ot_scr[u, h * hd:(h + 1) * hd, :] = _weighted_values_t(v_aug, p_scr[i % P_SLOTS])

    project = functools.partial(_project_out, x_ref=x_ref, ot_scr=ot_scr, wo_ref=wo_ref,
                                o_ref=o_ref)
    _pipelined(SUB_TILES * A_HEADS, scores, probs, values,
               _project_late(SUB_TILES, A_HEADS, project))


def _attn_a(x, q, k, vt, wo):
    b, seq, _ = x.shape
    rows = SUB_TILES * TQ
    tile = pl.BlockSpec((1, rows, D_MODEL), lambda i, j: (i, j, 0))
    return pl.pallas_call(
        _attn_a_kernel,
        grid=(b, seq // rows),
        in_specs=[tile, tile,
                  pl.BlockSpec((1, seq, A_KDUP_W), lambda i, j: (i, 0, 0)),
                  pl.BlockSpec((1, A_V_W, seq), lambda i, j: (i, 0, 0)),
                  _resident((D_MODEL, D_MODEL))],
        out_specs=tile,
        out_shape=jax.ShapeDtypeStruct(x.shape, F32),
        scratch_shapes=[pltpu.VMEM((SUB_TILES, D_MODEL, TQ), F32), *_stage_scratch(seq)],
        compiler_params=_params("parallel", "parallel"),
        name="attn_a",
    )(x, q, k, vt, wo)


def _pool_kernel(seq, x_ref, prev_ref, next_ref, g_ref, w_ref, sc_ref, o_ref, h_scr):
    i = pl.program_id(0)
    t0 = (i % (seq // TM)) * TM
    gain = g_ref[...]
    x = x_ref[...]
    h = _rms(x, gain)
    h_scr[0:POOL_HALO, :] = jnp.where(t0 > 0, _rms(prev_ref[...], gain), 0.0)
    h_scr[POOL_HALO:POOL_HALO + TM, :] = h
    h_scr[POOL_HALO + TM:, :] = jnp.where(t0 + TM < seq, _rms(next_ref[...], gain), 0.0)
    t = t0 + lax.broadcasted_iota(jnp.int32, (TM, 1), 0)
    ys = []
    for g, win in enumerate(POOL_WINDOWS):
        lanes = slice(g * POOL_GROUP, (g + 1) * POOL_GROUP)
        half = win // 2
        total = h_scr[POOL_HALO - half:POOL_HALO - half + TM, lanes]
        for d in range(1, win):
            total = total + h_scr[POOL_HALO - half + d:POOL_HALO - half + d + TM, lanes]
        lo = jnp.clip(t - half, 0, seq - 1)
        hi = jnp.clip(t + half - 1, 0, seq - 1)
        cnt = (hi - lo + 1).astype(F32)
        p = (total / cnt - h[:, lanes]).astype(BF16)
        ys.append(_dot(p, w_ref[g]))
    o_ref[...] = x + jnp.concatenate(ys, axis=1) * sc_ref[...]


def _pool(x, gain, w, scale, seq):
    n = x.shape[0]
    halo_per_tile = TM // POOL_HALO
    last_halo = n // POOL_HALO - 1
    row = pl.BlockSpec((TM, D_MODEL), lambda i: (i, 0))
    prev = pl.BlockSpec((POOL_HALO, D_MODEL),
                        lambda i: (jnp.maximum(i * halo_per_tile - 1, 0), 0))
    nxt = pl.BlockSpec((POOL_HALO, D_MODEL),
                       lambda i: (jnp.minimum((i + 1) * halo_per_tile, last_halo), 0))
    ng = len(POOL_WINDOWS)
    return pl.pallas_call(
        functools.partial(_pool_kernel, seq),
        grid=(n // TM,),
        in_specs=[row, prev, nxt, _resident((1, D_MODEL)),
                  _resident((ng, POOL_GROUP, POOL_GROUP)), _resident((1, D_MODEL))],
        out_specs=row,
        out_shape=jax.ShapeDtypeStruct((n, D_MODEL), F32),
        scratch_shapes=[pltpu.VMEM((TM + 2 * POOL_HALO, D_MODEL), F32)],
        compiler_params=_params("parallel"),
        name="pool",
    )(x, x, x, gain, w, scale)


def _qkv_c_kernel(x_ref, g_ref, w_ref, wvt_ref, q_ref, k_ref, vt_ref):
    h = _rms(x_ref[...], g_ref[...]).astype(BF16)
    width = 2 * C_HEADS * C_HEAD_DIM
    qk = _dot(h, w_ref[...])
    q_ref[...] = (qk[:, :width] * (C_HEAD_DIM ** -0.5 * LOG2E)).astype(BF16)
    k_ref[...] = qk[:, width:].astype(BF16)
    vt_ref[0] = _dot_nt(wvt_ref[...], h).astype(BF16)


def _qkv_c(x, gain, w, wvt, seq):
    n = x.shape[0]
    spt = seq // TM
    row = pl.BlockSpec((TM, D_MODEL), lambda i: (i, 0))
    out = jax.ShapeDtypeStruct((n, D_MODEL), BF16)
    return pl.pallas_call(
        _qkv_c_kernel,
        grid=(n // TM,),
        in_specs=[row, _resident((1, D_MODEL)), _resident((D_MODEL, 2 * D_MODEL)),
                  _resident((D_MODEL, D_MODEL))],
        out_specs=[row, row, pl.BlockSpec((1, D_MODEL, TM), lambda i: (i // spt, 0, i % spt))],
        out_shape=[out, out, jax.ShapeDtypeStruct((n // seq, D_MODEL, seq), BF16)],
        compiler_params=_params("parallel"),
        name="qkv_c",
    )(x, gain, w, wvt)


def _lane_pattern(lane, values):
    out = jnp.zeros(lane.shape, F32)
    for l, v in enumerate(values):
        out = jnp.where(lane == l, v, out)
    return out


def _attn_c_kernel(lam_init, nb, x_ref, q_ref, *refs):
    k_refs, vt_refs = refs[:nb], refs[nb:2 * nb]
    (lp_ref, sg_ref, wo_ref, o_ref, ot_scr, s_scr, p_scr, kaug_scr, qaug_scr,
     bias_scr) = refs[2 * nb:]
    probs = functools.partial(_probs_t, s_scr, p_scr)
    hw = 2 * C_HEAD_DIM
    first = pl.program_id(1) * SUB_TILES
    lp = lp_ref[...]
    lam = (jnp.exp(jnp.sum(lp[0:1] * lp[1:2], axis=-1, keepdims=True))
           - jnp.exp(jnp.sum(lp[2:3] * lp[3:4], axis=-1, keepdims=True)) + lam_init)

    lane = lax.broadcasted_iota(jnp.int32, (TQ, LANES), 1)
    offset = lax.broadcasted_iota(jnp.int32, (TQ, LANES), 0).astype(F32)
    e_parts = list(ALIBI_E_PARTS)
    for d in range(nb):
        j0 = (((first + d) % nb) * TQ).astype(F32)
        pattern = _lane_pattern(lane, 2 * e_parts + 3 * [j0] + 3 * [offset])
        wrapped = first + d >= nb
        for u in range(SUB_TILES):
            if d == u:
                kaug_scr[u, d] = jnp.zeros((TQ, LANES), BF16)
            else:
                sigma = 1.0 if d < u else jnp.where(wrapped, 1.0, -1.0)
                kaug_scr[u, d] = (sigma * pattern).astype(BF16)
    for u in range(SUB_TILES):
        i0 = ((first + u) * TQ).astype(F32)
        q_aug0 = _lane_pattern(lane, 3 * [-offset] + 3 * [-i0] + 2 * e_parts)
        for h in range(C_HEADS):
            qaug_scr[u, h] = (q_aug0 * 2.0 ** -h).astype(BF16)
    own_dist = jnp.abs(lax.broadcasted_iota(jnp.int32, (TQ, TQ), 0)
                       - lax.broadcasted_iota(jnp.int32, (TQ, TQ), 1)).astype(F32)
    for h in range(C_HEADS):
        bias_scr[h] = own_dist * (ALIBI_E * 2.0 ** -h)
    ones = jnp.ones((BF16_ROWS, nb * TQ), BF16)
    per_half = nb // KEY_SPLITS
    assert SUB_TILES <= per_half

    def scores(i):
        u, rest = divmod(i, 2 * C_HEADS)
        h, c = divmod(rest, 2)
        lanes = slice(h * hw, (h + 1) * hw)
        q_pair = q_ref[0, u * TQ:(u + 1) * TQ, lanes]
        q_c = jnp.where(_half_lanes_mask(TQ, c == 1), q_pair, jnp.zeros_like(q_pair))
        q_full = jnp.concatenate([q_c, qaug_scr[u, h]], axis=1)
        k_halves = [
            jnp.concatenate(
                [jnp.concatenate([k_refs[d][0, :, lanes], kaug_scr[u, d]], axis=1)
                 for d in range(r * per_half, (r + 1) * per_half)], axis=0)
            for r in range(KEY_SPLITS)]
        return _park_scores(s_scr, i, *_scores_t(k_halves, q_full, bias_scr[h], u * TQ))

    def values(i, _):
        u, rest = divmod(i, 2 * C_HEADS)
        h, c = divmod(rest, 2)
        v_t = jnp.concatenate([vt_refs[d][0, h * hw:(h + 1) * hw, :] for d in range(nb)], axis=1)
        v_aug = jnp.concatenate([v_t, ones], axis=0)
        o_c = _weighted_values_t(v_aug, p_scr[i % P_SLOTS])
        rows = slice(h * hw, (h + 1) * hw)
        if c == 0:
            ot_scr[u, rows, :] = o_c
        else:
            o = ot_scr[u, rows, :] - lam * o_c
            ms = jnp.mean(o * o, axis=0, keepdims=True)
            ot_scr[u, rows, :] = o * lax.rsqrt(ms + NORM_EPS) * sg_ref[...] * (1.0 - lam_init)

    project = functools.partial(_project_out, x_ref=x_ref, ot_scr=ot_scr, wo_ref=wo_ref,
                                o_ref=o_ref)
    _pipelined(SUB_TILES * 2 * C_HEADS, scores, probs, values,
               _project_late(SUB_TILES, 2 * C_HEADS, project))


def _attn_c(x, q, k, vt, lam_params, sub_gain_t, wo, lam_init):
    b, seq, _ = x.shape
    nb = seq // TQ
    rows = SUB_TILES * TQ
    tile = pl.BlockSpec((1, rows, D_MODEL), lambda i, j: (i, j, 0))
    k_blocks = [pl.BlockSpec((1, TQ, D_MODEL),
                             lambda i, j, d=d: (i, (j * SUB_TILES + d) % nb, 0))
                for d in range(nb)]
    vt_blocks = [pl.BlockSpec((1, D_MODEL, TQ),
                              lambda i, j, d=d: (i, 0, (j * SUB_TILES + d) % nb))
                 for d in range(nb)]
    return pl.pallas_call(
        functools.partial(_attn_c_kernel, lam_init, nb),
        grid=(b, seq // rows),
        in_specs=[tile, tile, *k_blocks, *vt_blocks,
                  _resident((4, C_HEAD_DIM)), _resident((2 * C_HEAD_DIM, TQ)),
                  _resident((D_MODEL, D_MODEL))],
        out_specs=tile,
        out_shape=jax.ShapeDtypeStruct(x.shape, F32),
        scratch_shapes=[pltpu.VMEM((SUB_TILES, D_MODEL, TQ), F32), *_stage_scratch(seq),
                        pltpu.VMEM((SUB_TILES, nb, TQ, LANES), BF16),
                        pltpu.VMEM((SUB_TILES, C_HEADS, TQ, LANES), BF16),
                        pltpu.VMEM((C_HEADS, TQ, TQ), F32)],
        compiler_params=_params("parallel", "parallel"),
        name="attn_c",
    )(x, q, *([k] * nb), *([vt] * nb), lam_params, sub_gain_t, wo)


F_EXTRA = BF16_ROWS


def _fourier_kernel(xd_ref, xm_ref, h_ref, ct_ref, st_ref, ctx_ref, stx_ref, cc_ref, sc_ref,
                    rev_ref, wo_ref, o_ref):
    h = h_ref[0]
    ct = jnp.concatenate([ct_ref[...], ctx_ref[...]], axis=0)
    st = jnp.concatenate([st_ref[...], stx_ref[...]], axis=0)
    p = _dot(ct, h).astype(BF16)
    q = _dot(st, h).astype(BF16)
    direct, mirror = [], []
    for g in range(F_GROUPS):
        lanes = slice(g * F_GROUP_DIM, (g + 1) * F_GROUP_DIM)
        a = _dot(p[:, lanes], cc_ref[...])
        b = _dot(q[:, lanes], sc_ref[...])
        direct.append(a[:TM] - b[:TM])
        mirror.append(a + b)
    f = jnp.concatenate(direct, axis=1).astype(BF16)
    o_ref[0, 0, 0] = xd_ref[0, 0] + _dot(f, wo_ref[...])
    fm = jnp.concatenate(mirror, axis=1).astype(BF16)
    rev = _dot(rev_ref[...], fm[:TM])
    row = lax.broadcasted_iota(jnp.int32, (TM, D_MODEL), 0)
    fm_rev = jnp.where(row == 0, fm[TM:TM + 1].astype(F32), rev).astype(BF16)
    o_ref[0, 0, 1] = xm_ref[0, 0] + _dot(fm_rev, wo_ref[...])


def _fourier(x, h, ct, st, cc, sc, rev, wo):
    b, seq, _ = x.shape
    nt = seq // TM
    assert nt == 4
    x4 = x.reshape(b, nt, TM, D_MODEL)
    tile = lambda f: pl.BlockSpec((1, 1, TM, D_MODEL), lambda i, j: (i, f(j), 0, 0))
    whole = pl.BlockSpec((1, seq, D_MODEL), lambda i, j: (i, 0, 0))
    tab = pl.BlockSpec((TM, seq), lambda i, j: (j, 0))
    tab_extra = pl.BlockSpec((F_EXTRA, seq), lambda i, j: ((j + 1) * (TM // F_EXTRA), 0))
    square = _resident((F_GROUP_DIM, F_GROUP_DIM))
    return pl.pallas_call(
        _fourier_kernel,
        grid=(b, nt // 2),
        in_specs=[tile(lambda j: j), tile(lambda j: nt - 1 - j), whole, tab, tab, tab_extra,
                  tab_extra, square, square, _resident((TM, TM)), _resident((D_MODEL, D_MODEL))],
        out_specs=pl.BlockSpec((1, 1, 2, TM, D_MODEL), lambda i, j: (i, j, 0, 0, 0)),
        out_shape=jax.ShapeDtypeStruct((b, nt // 2, 2, TM, D_MODEL), F32),
        compiler_params=_params("parallel", "parallel"),
        name="fourier",
    )(x4, x4, h, ct, st, ct, st, cc, sc, rev, wo)


def _fourier_slot(t):
    return jnp.where(t < 2, 2 * t, 7 - 2 * t)


def _rope_tables(seq):
    rows = seq // GRID_W
    r, c = jnp.meshgrid(jnp.arange(rows), jnp.arange(GRID_W), indexing="ij")
    r = r.reshape(-1).astype(F32)
    c = c.reshape(-1).astype(F32)
    n = A_HEAD_DIM // 4
    freqs = ROPE_THETA ** (-jnp.arange(n, dtype=F32) / n)
    ang = jnp.concatenate([r[:, None] * freqs, c[:, None] * freqs], axis=-1)
    cos = jnp.repeat(jnp.cos(ang), 2, axis=-1)
    sin = jnp.repeat(jnp.sin(ang), 2, axis=-1)
    sign = jnp.where(jnp.arange(A_HEAD_DIM) % 2 == 0, -1.0, 1.0).astype(F32)
    sin = sin * sign
    reps = LANES // A_HEAD_DIM
    return jnp.tile(cos, (1, reps)), jnp.tile(sin, (1, reps))


def _dft_tables(n, scale, rows=None):
    k = jnp.arange(rows or n, dtype=jnp.int32)
    kn = (k[:, None] * jnp.arange(n, dtype=jnp.int32)[None, :]) % n
    ang = kn.astype(F32) * (2.0 * math.pi / n)
    return (jnp.cos(ang) * scale).astype(BF16), (jnp.sin(ang) * scale).astype(BF16)


def _reversal_matrix():
    r = jnp.arange(TM)
    return (r[:, None] + r[None, :] == TM).astype(BF16)


def _head_reduce_tables():
    lane_head = jnp.arange(A_QK_W) // A_HEAD_DIM
    onehot = (lane_head[:, None] == jnp.arange(LANES)[None, :]).astype(F32)
    expand = onehot.T.astype(BF16)
    return (onehot / A_HEAD_DIM).astype(BF16), jnp.concatenate([expand, expand], axis=0)


def _trunk(x, p):
    b, seq, _ = x.shape
    n = b * seq
    flat = lambda a: a.reshape(n, a.shape[-1])
    per_seq = lambda a: a.reshape(b, seq, a.shape[-1])
    x = flat(x)
    for i in LAYERS:
        kind = i % 4
        gain = p["attn_norm"][i][None]
        ffn_args = (p["ffn_norm"][i][None], p["w_gate"][i], p["w_up"][i], p["w_down"][i])
        if kind == 0:
            q, k, vt = _qkv_a(x, gain, p["a_w_qk"], p["a_w_vt"], p["e"], p["et"],
                              p["a_head_gain"], p["cos"], p["sin"], seq)
            x = flat(_attn_a(per_seq(x), per_seq(q), per_seq(k), vt, p["a_w_o"]))
            x = _ffn(x, *ffn_args, p["final_norm"], "plain")
        elif kind == 1:
            x = _pool(x, gain, p["b_w_pool"], p["b_scale"], seq)
            x = _ffn(x, *ffn_args, p["final_norm"], "plain")
        elif kind == 2:
            lam_init = 0.8 - 0.6 * math.exp(-0.3 * i)
            q, k, vt = _qkv_c(x, gain, p["c_w_qk"], p["c_w_vt"], seq)
            x = flat(_attn_c(per_seq(x), per_seq(q), per_seq(k), vt, p["c_lambda"],
                             p["c_sub_gain_t"], p["c_w_o"], lam_init))
            x, h = _ffn(x, *ffn_args, p["attn_norm"][i + 1][None], "emit")
        else:
            assert i == DEPTH - 1
            x = _fourier(per_seq(x), per_seq(h), p["ct"], p["st"], p["cc"], p["sc"],
                         p["rev"], p["d_w_o"]).reshape(n, D_MODEL)
            tiles = seq // TM
            x = _ffn(x, *ffn_args, p["final_norm"], "final",
                     in_tile=lambda t: (t // tiles) * tiles + _fourier_slot(t % tiles))
    return per_seq(x)


def kernel(x_prompt, x_sample, attn_norm, ffn_norm, final_norm, a_w_qkv, a_q_gain, a_k_gain,
           a_w_o, b_w_pool, b_scale, c_w_qkv, c_lambda, c_sub_gain, c_w_o, d_w_o,
           w_gate, w_up, w_down):
    seq = x_prompt.shape[1]
    assert x_sample.shape[1] == seq and seq % FFN_TM == 0 and seq % (SUB_TILES * TQ) == 0
    e, et = _head_reduce_tables()
    cos, sin = _rope_tables(seq)
    ct, st = _dft_tables(seq, seq ** -0.5, rows=seq // 2 + F_EXTRA)
    cc, sc = _dft_tables(F_GROUP_DIM, F_GROUP_DIM ** -0.5)
    wa = a_w_qkv[0]
    a_w_qk = wa[:, :A_QK_W]
    head_gain = jnp.concatenate([jnp.tile(a_q_gain[0], A_HEADS) * (A_HEAD_DIM ** -0.5 * LOG2E),
                                 jnp.tile(a_k_gain[0], A_KV_HEADS)])[None]
    wc = c_w_qkv[0]
    p = dict(
        attn_norm=attn_norm, ffn_norm=ffn_norm, final_norm=final_norm[None],
        a_w_qk=a_w_qk.astype(BF16), a_w_vt=wa[:, A_Q_W + A_V_W:].T.astype(BF16),
        a_head_gain=head_gain, a_w_o=a_w_o[0].astype(BF16),
        e=e, et=et, cos=cos, sin=sin,
        b_w_pool=b_w_pool[0].astype(BF16), b_scale=b_scale[0][None],
        c_w_qk=wc[:, :2 * D_MODEL].astype(BF16), c_w_vt=wc[:, 2 * D_MODEL:].T.astype(BF16),
        c_lambda=c_lambda[0],
        c_sub_gain_t=jnp.broadcast_to(c_sub_gain[0][:, None], (2 * C_HEAD_DIM, TQ)),
        c_w_o=c_w_o[0].astype(BF16),
        d_w_o=d_w_o[0].astype(BF16), ct=ct, st=st, cc=cc, sc=sc, rev=_reversal_matrix(),
        w_gate=w_gate.astype(BF16), w_up=w_up.astype(BF16), w_down=w_down.astype(BF16),
    )
    return _trunk(x_prompt, p), _trunk(x_sample, p)
```

```python
import functools
import math

import jax
import jax.numpy as jnp
import ml_dtypes
from jax import lax
from jax.experimental import pallas as pl
from jax.experimental.pallas import tpu as pltpu

F32 = jnp.float32
BF16 = jnp.bfloat16

D_MODEL = 1024
DEPTH = 4
LAYERS = tuple(range(DEPTH))
GRID_W = 64
NORM_EPS = 1e-6
A_HEADS = 16
A_KV_HEADS = 4
A_HEAD_DIM = 64
A_GROUP = A_HEADS // A_KV_HEADS
ROPE_THETA = 10000.0
POOL_WINDOWS = (2, 4, 8, 16)
POOL_GROUP = D_MODEL // len(POOL_WINDOWS)
POOL_HALO = 8
C_HEADS = 8
C_HEAD_DIM = 64
F_GROUPS = 4
F_GROUP_DIM = D_MODEL // F_GROUPS
D_FF = 2816
LOG2E = math.log2(math.e)


def _bf16_parts(x, n):
    parts = []
    for _ in range(n):
        parts.append(float(ml_dtypes.bfloat16(x)))
        x -= parts[-1]
    return tuple(parts)


assert C_HEADS == 8
ALIBI_E = 2.0 ** (-8.0 / C_HEADS) * LOG2E
ALIBI_E_PARTS = _bf16_parts(ALIBI_E, 3)

LANES = 128
BF16_ROWS = 16
VMEM_LIMIT = 56 * 1024 * 1024

A_Q_W = A_HEADS * A_HEAD_DIM
A_V_W = A_KV_HEADS * A_HEAD_DIM
A_QK_W = A_Q_W + A_V_W
A_KDUP_W = A_KV_HEADS * LANES

TM = 512
FFN_TM = 2 * TM
TQ = 256
SUB_TILES = 2
OUT_DELAY = 2
FF_CHUNK = 256
KEY_SPLITS = 2
C_KEY_SPLITS = 1


def _params(*sem):
    return pltpu.CompilerParams(dimension_semantics=sem, vmem_limit_bytes=VMEM_LIMIT)


def _resident(shape):
    nd = len(shape)
    return pl.BlockSpec(shape, lambda *_: (0,) * nd, pipeline_mode=pl.Buffered(1))


def _rms(x, gain):
    ms = jnp.mean(x * x, axis=-1, keepdims=True)
    return x * lax.rsqrt(ms + NORM_EPS) * gain


def _split_bf16(x):
    hi = x.astype(BF16)
    lo = (x - hi.astype(F32)).astype(BF16)
    return hi, lo


def _dot(a, b):
    return jnp.dot(a, b, preferred_element_type=F32)


def _dot_nt(a, b):
    return lax.dot_general(a, b, (((1,), (1,)), ((), ())), preferred_element_type=F32)


def _half_lanes_mask(rows, upper):
    lane = lax.broadcasted_iota(jnp.int32, (rows, LANES), 1)
    return lane >= LANES // 2 if upper else lane < LANES // 2


def _scores_t(k_halves, q, own_bias=None, own_row=0):
    halves = [_dot_nt(k, q) for k in k_halves]
    if own_bias is not None:
        first, end = halves[0], own_row + own_bias.shape[0]
        pieces = [first[:own_row], first[own_row:end] - own_bias, first[end:]]
        halves[0] = jnp.concatenate([piece for piece in pieces if piece.shape[0]], axis=0)
    m = functools.reduce(jnp.maximum, [jnp.max(s_t, axis=0, keepdims=True) for s_t in halves])
    return halves, m


S_SLOTS = 3
P_SLOTS = 2


def _stage_scratch(keys):
    return [pltpu.VMEM((S_SLOTS, keys, TQ), F32), pltpu.VMEM((P_SLOTS, keys, TQ), BF16)]


def _park_scores(s_scr, i, halves, m):
    part = halves[0].shape[0]
    for r, s_t in enumerate(halves):
        s_scr[i % s_scr.shape[0], r * part:(r + 1) * part, :] = s_t
    return i, m


def _probs_t(s_scr, p_scr, i, m):
    p_scr[i % p_scr.shape[0]] = jnp.exp2(s_scr[i % s_scr.shape[0]] - m).astype(BF16)
    return i


def _weighted_values_t(v_aug, p_t):
    d = v_aug.shape[0] - BF16_ROWS
    oa = _dot(v_aug, p_t)
    return oa[:d] / oa[d:d + 1]


def _pipelined(n, scores, probs, values, after=None):
    s = {0: scores(0)}
    if n > 1:
        s[1] = scores(1)
    p = {0: probs(*s.pop(0))}
    for i in range(n):
        if i + 2 < n:
            s[i + 2] = scores(i + 2)
        if i + 1 < n:
            p[i + 1] = probs(*s.pop(i + 1))
        values(i, p.pop(i))
        if after is not None:
            after(i)


def _project_late(n_sub, items_per_sub, project):
    done = []

    def after(i):
        for u in range(n_sub):
            ready = (u + 1) * items_per_sub - 1 + OUT_DELAY
            if u not in done and i >= min(ready, n_sub * items_per_sub - 1):
                done.append(u)
                project(u)
    return after


def _ffn_apply(x, g_ref, wg_ref, wu_ref, wd_ref):
    h = _rms(x, g_ref[...]).astype(BF16)
    acc = x
    for c in range(D_FF // FF_CHUNK):
        cols = slice(c * FF_CHUNK, (c + 1) * FF_CHUNK)
        g = _dot(h, wg_ref[:, cols])
        u = _dot(h, wu_ref[:, cols])
        a = (g * jax.nn.sigmoid(g) * u).astype(BF16)
        acc = acc + _dot(a, wd_ref[cols, :])
    return acc


def _ffn_specs():
    return [_resident((1, D_MODEL)), _resident((D_MODEL, D_FF)), _resident((D_MODEL, D_FF)),
            _resident((D_FF, D_MODEL))]


def _ffn_kernel(mode, x_ref, g_ref, wg_ref, wu_ref, wd_ref, ng_ref, o_ref, *h_ref):
    acc = _ffn_apply(x_ref[...], g_ref, wg_ref, wu_ref, wd_ref)
    if mode == "plain":
        o_ref[...] = acc
        return
    normed = _rms(acc, ng_ref[...])
    o_ref[...] = normed if mode == "final" else acc
    if mode == "emit":
        h_ref[0][...] = normed.astype(BF16)


def _ffn(x, gain, wg, wu, wd, next_gain, mode, in_tile=lambda i: i):
    n = x.shape[0]
    tm = TM if mode == "final" else FFN_TM
    row = pl.BlockSpec((tm, D_MODEL), lambda i: (i, 0))
    out = jax.ShapeDtypeStruct((n, D_MODEL), F32)
    emit = mode == "emit"
    return pl.pallas_call(
        functools.partial(_ffn_kernel, mode),
        grid=(n // tm,),
        in_specs=[pl.BlockSpec((tm, D_MODEL), lambda i: (in_tile(i), 0)), *_ffn_specs(),
                  _resident((1, D_MODEL))],
        out_specs=[row, row] if emit else row,
        out_shape=[out, jax.ShapeDtypeStruct((n, D_MODEL), BF16)] if emit else out,
        compiler_params=_params("parallel"),
        name="ffn_" + mode,
    )(x, gain, wg, wu, wd, next_gain)


def _qkv_a_kernel(x_ref, g_ref, w_ref, wvt_ref, e_ref, et_ref, hg_ref, cos_ref, sin_ref,
                  q_ref, k_ref, vt_ref):
    h = _rms(x_ref[...], g_ref[...]).astype(BF16)
    qk = _dot(h, w_ref[...])
    sq_hi, sq_lo = _split_bf16(qk * qk)
    ms = _dot(sq_hi, e_ref[...]) + _dot(sq_lo, e_ref[...])
    rs = _dot(jnp.concatenate(_split_bf16(lax.rsqrt(ms + NORM_EPS)), axis=1), et_ref[...])
    qn = qk * rs * hg_ref[...]
    lane = lax.broadcasted_iota(jnp.int32, qn.shape, 1)
    partner = jnp.where(lane % 2 == 0, pltpu.roll(qn, A_QK_W - 1, 1), pltpu.roll(qn, 1, 1))
    reps = A_QK_W // LANES
    cos = jnp.concatenate([cos_ref[...]] * reps, axis=1)
    sin = jnp.concatenate([sin_ref[...]] * reps, axis=1)
    out = qn * cos + partner * sin
    q_ref[...] = out[:, :A_Q_W].astype(BF16)
    kk = out[:, A_Q_W:]
    up = pltpu.roll(kk, A_HEAD_DIM, 1)
    down = pltpu.roll(kk, A_V_W - A_HEAD_DIM, 1)
    lower = _half_lanes_mask(kk.shape[0], False)
    for g in range(A_KV_HEADS):
        src = slice((g // 2) * LANES, (g // 2 + 1) * LANES)
        pair = (jnp.where(lower, kk[:, src], up[:, src]) if g % 2 == 0
                else jnp.where(lower, down[:, src], kk[:, src]))
        k_ref[:, g * LANES:(g + 1) * LANES] = pair.astype(BF16)
    vt_ref[0] = _dot_nt(wvt_ref[...], h).astype(BF16)


def _qkv_a(x, gain, w, wvt, e, et, head_gain, cos, sin, seq):
    n = x.shape[0]
    spt = seq // TM
    row = lambda width: pl.BlockSpec((TM, width), lambda i: (i, 0))
    tab = pl.BlockSpec((TM, LANES), lambda i: (i % spt, 0))
    return pl.pallas_call(
        _qkv_a_kernel,
        grid=(n // TM,),
        in_specs=[row(D_MODEL), _resident((1, D_MODEL)), _resident((D_MODEL, A_QK_W)),
                  _resident((A_V_W, D_MODEL)), _resident((A_QK_W, LANES)),
                  _resident((2 * LANES, A_QK_W)), _resident((1, A_QK_W)), tab, tab],
        out_specs=[row(A_Q_W), row(A_KDUP_W),
                   pl.BlockSpec((1, A_V_W, TM), lambda i: (i // spt, 0, i % spt))],
        out_shape=[jax.ShapeDtypeStruct((n, A_Q_W), BF16),
                   jax.ShapeDtypeStruct((n, A_KDUP_W), BF16),
                   jax.ShapeDtypeStruct((n // seq, A_V_W, seq), BF16)],
        compiler_params=_params("parallel"),
        name="qkv_a",
    )(x, gain, w, wvt, e, et, head_gain, cos, sin)


def _project_out(u, x_ref, ot_scr, wo_ref, o_ref):
    rows = slice(u * TQ, (u + 1) * TQ)
    o = ot_scr[u].T.astype(BF16)
    o_ref[0, rows] = x_ref[0, rows] + _dot(o, wo_ref[...])


def _attn_a_kernel(x_ref, q_ref, k_ref, vt_ref, wo_ref, o_ref, ot_scr, s_scr, p_scr):
    seq = k_ref.shape[1]
    hd = A_HEAD_DIM
    ones = jnp.ones((BF16_ROWS, seq), BF16)
    probs = functools.partial(_probs_t, s_scr, p_scr)

    def scores(i):
        u, h = divmod(i, A_HEADS)
        g = h // A_GROUP
        q_pair = q_ref[0, u * TQ:(u + 1) * TQ, (h // 2) * LANES:(h // 2 + 1) * LANES]
        q_h = jnp.where(_half_lanes_mask(TQ, h % 2 == 1), q_pair, jnp.zeros_like(q_pair))
        part = seq // KEY_SPLITS
        k_halves = [k_ref[0, r * part:(r + 1) * part, g * LANES:(g + 1) * LANES]
                    for r in range(KEY_SPLITS)]
        return _park_scores(s_scr, i, *_scores_t(k_halves, q_h))

    def values(i, _):
        u, h = divmod(i, A_HEADS)
        g = h // A_GROUP
        v_aug = jnp.concatenate([vt_ref[0, g * hd:(g + 1) * hd, :], ones], axis=0)
        ot_scr[u, h * hd:(h + 1) * hd, :] = _weighted_values_t(v_aug, p_scr[i % P_SLOTS])

    project = functools.partial(_project_out, x_ref=x_ref, ot_scr=ot_scr, wo_ref=wo_ref,
                                o_ref=o_ref)
    _pipelined(SUB_TILES * A_HEADS, scores, probs, values,
               _project_late(SUB_TILES, A_HEADS, project))


def _attn_a(x, q, k, vt, wo):
    b, seq, _ = x.shape
    rows = SUB_TILES * TQ
    tile = pl.BlockSpec((1, rows, D_MODEL), lambda i, j: (i, j, 0))
    return pl.pallas_call(
        _attn_a_kernel,
        grid=(b, seq // rows),
        in_specs=[tile, tile,
                  pl.BlockSpec((1, seq, A_KDUP_W), lambda i, j: (i, 0, 0)),
                  pl.BlockSpec((1, A_V_W, seq), lambda i, j: (i, 0, 0)),
                  _resident((D_MODEL, D_MODEL))],
        out_specs=tile,
        out_shape=jax.ShapeDtypeStruct(x.shape, F32),
        scratch_shapes=[pltpu.VMEM((SUB_TILES, D_MODEL, TQ), F32), *_stage_scratch(seq)],
        compiler_params=_params("parallel", "parallel"),
        name="attn_a",
    )(x, q, k, vt, wo)


def _pool_kernel(seq, x_ref, prev_ref, next_ref, g_ref, w_ref, sc_ref, o_ref, h_scr):
    i = pl.program_id(0)
    t0 = (i % (seq // TM)) * TM
    gain = g_ref[...]
    x = x_ref[...]
    h = _rms(x, gain)
    h_scr[0:POOL_HALO, :] = jnp.where(t0 > 0, _rms(prev_ref[...], gain), 0.0)
    h_scr[POOL_HALO:POOL_HALO + TM, :] = h
    h_scr[POOL_HALO + TM:, :] = jnp.where(t0 + TM < seq, _rms(next_ref[...], gain), 0.0)
    t = t0 + lax.broadcasted_iota(jnp.int32, (TM, 1), 0)
    ys = []
    for g, win in enumerate(POOL_WINDOWS):
        lanes = slice(g * POOL_GROUP, (g + 1) * POOL_GROUP)
        half = win // 2
        total = h_scr[POOL_HALO - half:POOL_HALO - half + TM, lanes]
        for d in range(1, win):
            total = total + h_scr[POOL_HALO - half + d:POOL_HALO - half + d + TM, lanes]
        lo = jnp.clip(t - half, 0, seq - 1)
        hi = jnp.clip(t + half - 1, 0, seq - 1)
        cnt = (hi - lo + 1).astype(F32)
        p = (total / cnt - h[:, lanes]).astype(BF16)
        ys.append(_dot(p, w_ref[g]))
    o_ref[...] = x + jnp.concatenate(ys, axis=1) * sc_ref[...]


def _pool(x, gain, w, scale, seq):
    n = x.shape[0]
    halo_per_tile = TM // POOL_HALO
    last_halo = n // POOL_HALO - 1
    row = pl.BlockSpec((TM, D_MODEL), lambda i: (i, 0))
    prev = pl.BlockSpec((POOL_HALO, D_MODEL),
                        lambda i: (jnp.maximum(i * halo_per_tile - 1, 0), 0))
    nxt = pl.BlockSpec((POOL_HALO, D_MODEL),
                       lambda i: (jnp.minimum((i + 1) * halo_per_tile, last_halo), 0))
    ng = len(POOL_WINDOWS)
    return pl.pallas_call(
        functools.partial(_pool_kernel, seq),
        grid=(n // TM,),
        in_specs=[row, prev, nxt, _resident((1, D_MODEL)),
                  _resident((ng, POOL_GROUP, POOL_GROUP)), _resident((1, D_MODEL))],
        out_specs=row,
        out_shape=jax.ShapeDtypeStruct((n, D_MODEL), F32),
        scratch_shapes=[pltpu.VMEM((TM + 2 * POOL_HALO, D_MODEL), F32)],
        compiler_params=_params("parallel"),
        name="pool",
    )(x, x, x, gain, w, scale)


def _qkv_c_kernel(x_ref, g_ref, w_ref, wvt_ref, q_ref, k_ref, vt_ref):
    h = _rms(x_ref[...], g_ref[...]).astype(BF16)
    width = 2 * C_HEADS * C_HEAD_DIM
    qk = _dot(h, w_ref[...])
    q_ref[...] = (qk[:, :width] * (C_HEAD_DIM ** -0.5 * LOG2E)).astype(BF16)
    k_ref[...] = qk[:, width:].astype(BF16)
    vt_ref[0] = _dot_nt(wvt_ref[...], h).astype(BF16)


def _qkv_c(x, gain, w, wvt, seq):
    n = x.shape[0]
    spt = seq // TM
    row = pl.BlockSpec((TM, D_MODEL), lambda i: (i, 0))
    out = jax.ShapeDtypeStruct((n, D_MODEL), BF16)
    return pl.pallas_call(
        _qkv_c_kernel,
        grid=(n // TM,),
        in_specs=[row, _resident((1, D_MODEL)), _resident((D_MODEL, 2 * D_MODEL)),
                  _resident((D_MODEL, D_MODEL))],
        out_specs=[row, row, pl.BlockSpec((1, D_MODEL, TM), lambda i: (i // spt, 0, i % spt))],
        out_shape=[out, out, jax.ShapeDtypeStruct((n // seq, D_MODEL, seq), BF16)],
        compiler_params=_params("parallel"),
        name="qkv_c",
    )(x, gain, w, wvt)


def _lane_pattern(lane, values):
    out = jnp.zeros(lane.shape, F32)
    for l, v in enumerate(values):
        out = jnp.where(lane == l, v, out)
    return out


def _attn_c_kernel(lam_init, nb, x_ref, q_ref, *refs):
    k_refs, vt_refs = refs[:nb], refs[nb:2 * nb]
    (lp_ref, sg_ref, wo_ref, o_ref, ot_scr, s_scr, p_scr, kaug_scr, qaug_scr,
     bias_scr) = refs[2 * nb:]
    probs = functools.partial(_probs_t, s_scr, p_scr)
    hw = 2 * C_HEAD_DIM
    first = pl.program_id(1) * SUB_TILES
    lp = lp_ref[...]
    lam = (jnp.exp(jnp.sum(lp[0:1] * lp[1:2], axis=-1, keepdims=True))
           - jnp.exp(jnp.sum(lp[2:3] * lp[3:4], axis=-1, keepdims=True)) + lam_init)

    lane = lax.broadcasted_iota(jnp.int32, (TQ, LANES), 1)
    offset = lax.broadcasted_iota(jnp.int32, (TQ, LANES), 0).astype(F32)
    e_parts = list(ALIBI_E_PARTS)
    for d in range(nb):
        j0 = (((first + d) % nb) * TQ).astype(F32)
        pattern = _lane_pattern(lane, 2 * e_parts + 3 * [j0] + 3 * [offset])
        wrapped = first + d >= nb
        for u in range(SUB_TILES):
            if d == u:
                kaug_scr[u, d] = jnp.zeros((TQ, LANES), BF16)
            else:
                sigma = 1.0 if d < u else jnp.where(wrapped, 1.0, -1.0)
                kaug_scr[u, d] = (sigma * pattern).astype(BF16)
    for u in range(SUB_TILES):
        i0 = ((first + u) * TQ).astype(F32)
        q_aug0 = _lane_pattern(lane, 3 * [-offset] + 3 * [-i0] + 2 * e_parts)
        for h in range(C_HEADS):
            qaug_scr[u, h] = (q_aug0 * 2.0 ** -h).astype(BF16)
    own_dist = jnp.abs(lax.broadcasted_iota(jnp.int32, (TQ, TQ), 0)
                       - lax.broadcasted_iota(jnp.int32, (TQ, TQ), 1)).astype(F32)
    for h in range(C_HEADS):
        bias_scr[h] = own_dist * (ALIBI_E * 2.0 ** -h)
    ones = jnp.ones((BF16_ROWS, nb * TQ), BF16)
    per_half = nb // C_KEY_SPLITS
    assert SUB_TILES <= per_half

    def scores(i):
        u, rest = divmod(i, 2 * C_HEADS)
        h, c = divmod(rest, 2)
        lanes = slice(h * hw, (h + 1) * hw)
        q_pair = q_ref[0, u * TQ:(u + 1) * TQ, lanes]
        q_c = jnp.where(_half_lanes_mask(TQ, c == 1), q_pair, jnp.zeros_like(q_pair))
        q_full = jnp.concatenate([q_c, qaug_scr[u, h]], axis=1)
        k_halves = [
            jnp.concatenate(
                [jnp.concatenate([k_refs[d][0, :, lanes], kaug_scr[u, d]], axis=1)
                 for d in range(r * per_half, (r + 1) * per_half)], axis=0)
            for r in range(C_KEY_SPLITS)]
        return _park_scores(s_scr, i, *_scores_t(k_halves, q_full, bias_scr[h], u * TQ))

    def values(i, _):
        u, rest = divmod(i, 2 * C_HEADS)
        h, c = divmod(rest, 2)
        v_t = jnp.concatenate([vt_refs[d][0, h * hw:(h + 1) * hw, :] for d in range(nb)], axis=1)
        v_aug = jnp.concatenate([v_t, ones], axis=0)
        o_c = _weighted_values_t(v_aug, p_scr[i % P_SLOTS])
        rows = slice(h * hw, (h + 1) * hw)
        if c == 0:
            ot_scr[u, rows, :] = o_c
        else:
            o = ot_scr[u, rows, :] - lam * o_c
            ms = jnp.mean(o * o, axis=0, keepdims=True)
            ot_scr[u, rows, :] = o * lax.rsqrt(ms + NORM_EPS) * sg_ref[...] * (1.0 - lam_init)

    project = functools.partial(_project_out, x_ref=x_ref, ot_scr=ot_scr, wo_ref=wo_ref,
                                o_ref=o_ref)
    _pipelined(SUB_TILES * 2 * C_HEADS, scores, probs, values,
               _project_late(SUB_TILES, 2 * C_HEADS, project))


def _attn_c(x, q, k, vt, lam_params, sub_gain_t, wo, lam_init):
    b, seq, _ = x.shape
    nb = seq // TQ
    rows = SUB_TILES * TQ
    tile = pl.BlockSpec((1, rows, D_MODEL), lambda i, j: (i, j, 0))
    k_blocks = [pl.BlockSpec((1, TQ, D_MODEL),
                             lambda i, j, d=d: (i, (j * SUB_TILES + d) % nb, 0))
                for d in range(nb)]
    vt_blocks = [pl.BlockSpec((1, D_MODEL, TQ),
                              lambda i, j, d=d: (i, 0, (j * SUB_TILES + d) % nb))
                 for d in range(nb)]
    return pl.pallas_call(
        functools.partial(_attn_c_kernel, lam_init, nb),
        grid=(b, seq // rows),
        in_specs=[tile, tile, *k_blocks, *vt_blocks,
                  _resident((4, C_HEAD_DIM)), _resident((2 * C_HEAD_DIM, TQ)),
                  _resident((D_MODEL, D_MODEL))],
        out_specs=tile,
        out_shape=jax.ShapeDtypeStruct(x.shape, F32),
        scratch_shapes=[pltpu.VMEM((SUB_TILES, D_MODEL, TQ), F32), *_stage_scratch(seq),
                        pltpu.VMEM((SUB_TILES, nb, TQ, LANES), BF16),
                        pltpu.VMEM((SUB_TILES, C_HEADS, TQ, LANES), BF16),
                        pltpu.VMEM((C_HEADS, TQ, TQ), F32)],
        compiler_params=_params("parallel", "parallel"),
        name="attn_c",
    )(x, q, *([k] * nb), *([vt] * nb), lam_params, sub_gain_t, wo)


F_EXTRA = BF16_ROWS


def _fourier_kernel(xd_ref, xm_ref, h_ref, ct_ref, st_ref, ctx_ref, stx_ref, cc_ref, sc_ref,
                    rev_ref, wo_ref, o_ref):
    h = h_ref[0]
    ct = jnp.concatenate([ct_ref[...], ctx_ref[...]], axis=0)
    st = jnp.concatenate([st_ref[...], stx_ref[...]], axis=0)
    p = _dot(ct, h).astype(BF16)
    q = _dot(st, h).astype(BF16)
    direct, mirror = [], []
    for g in range(F_GROUPS):
        lanes = slice(g * F_GROUP_DIM, (g + 1) * F_GROUP_DIM)
        a = _dot(p[:, lanes], cc_ref[...])
        b = _dot(q[:, lanes], sc_ref[...])
        direct.append(a[:TM] - b[:TM])
        mirror.append(a + b)
    f = jnp.concatenate(direct, axis=1).astype(BF16)
    o_ref[0, 0, 0] = xd_ref[0, 0] + _dot(f, wo_ref[...])
    fm = jnp.concatenate(mirror, axis=1).astype(BF16)
    rev = _dot(rev_ref[...], fm[:TM])
    row = lax.broadcasted_iota(jnp.int32, (TM, D_MODEL), 0)
    fm_rev = jnp.where(row == 0, fm[TM:TM + 1].astype(F32), rev).astype(BF16)
    o_ref[0, 0, 1] = xm_ref[0, 0] + _dot(fm_rev, wo_ref[...])


def _fourier(x, h, ct, st, cc, sc, rev, wo):
    b, seq, _ = x.shape
    nt = seq // TM
    assert nt == 4
    x4 = x.reshape(b, nt, TM, D_MODEL)
    tile = lambda f: pl.BlockSpec((1, 1, TM, D_MODEL), lambda i, j: (i, f(j), 0, 0))
    whole = pl.BlockSpec((1, seq, D_MODEL), lambda i, j: (i, 0, 0))
    tab = pl.BlockSpec((TM, seq), lambda i, j: (j, 0))
    tab_extra = pl.BlockSpec((F_EXTRA, seq), lambda i, j: ((j + 1) * (TM // F_EXTRA), 0))
    square = _resident((F_GROUP_DIM, F_GROUP_DIM))
    return pl.pallas_call(
        _fourier_kernel,
        grid=(b, nt // 2),
        in_specs=[tile(lambda j: j), tile(lambda j: nt - 1 - j), whole, tab, tab, tab_extra,
                  tab_extra, square, square, _resident((TM, TM)), _resident((D_MODEL, D_MODEL))],
        out_specs=pl.BlockSpec((1, 1, 2, TM, D_MODEL), lambda i, j: (i, j, 0, 0, 0)),
        out_shape=jax.ShapeDtypeStruct((b, nt // 2, 2, TM, D_MODEL), F32),
        compiler_params=_params("parallel", "parallel"),
        name="fourier",
    )(x4, x4, h, ct, st, ct, st, cc, sc, rev, wo)


def _fourier_slot(t):
    return jnp.where(t < 2, 2 * t, 7 - 2 * t)


def _rope_tables(seq):
    rows = seq // GRID_W
    r, c = jnp.meshgrid(jnp.arange(rows), jnp.arange(GRID_W), indexing="ij")
    r = r.reshape(-1).astype(F32)
    c = c.reshape(-1).astype(F32)
    n = A_HEAD_DIM // 4
    freqs = ROPE_THETA ** (-jnp.arange(n, dtype=F32) / n)
    ang = jnp.concatenate([r[:, None] * freqs, c[:, None] * freqs], axis=-1)
    cos = jnp.repeat(jnp.cos(ang), 2, axis=-1)
    sin = jnp.repeat(jnp.sin(ang), 2, axis=-1)
    sign = jnp.where(jnp.arange(A_HEAD_DIM) % 2 == 0, -1.0, 1.0).astype(F32)
    sin = sin * sign
    reps = LANES // A_HEAD_DIM
    return jnp.tile(cos, (1, reps)), jnp.tile(sin, (1, reps))


def _dft_tables(n, scale, rows=None):
    k = jnp.arange(rows or n, dtype=jnp.int32)
    kn = (k[:, None] * jnp.arange(n, dtype=jnp.int32)[None, :]) % n
    ang = kn.astype(F32) * (2.0 * math.pi / n)
    return (jnp.cos(ang) * scale).astype(BF16), (jnp.sin(ang) * scale).astype(BF16)


def _reversal_matrix():
    r = jnp.arange(TM)
    return (r[:, None] + r[None, :] == TM).astype(BF16)


def _head_reduce_tables():
    lane_head = jnp.arange(A_QK_W) // A_HEAD_DIM
    onehot = (lane_head[:, None] == jnp.arange(LANES)[None, :]).astype(F32)
    expand = onehot.T.astype(BF16)
    return (onehot / A_HEAD_DIM).astype(BF16), jnp.concatenate([expand, expand], axis=0)


def _trunk(x, p):
    b, seq, _ = x.shape
    n = b * seq
    flat = lambda a: a.reshape(n, a.shape[-1])
    per_seq = lambda a: a.reshape(b, seq, a.shape[-1])
    x = flat(x)
    for i in LAYERS:
        kind = i % 4
        gain = p["attn_norm"][i][None]
        ffn_args = (p["ffn_norm"][i][None], p["w_gate"][i], p["w_up"][i], p["w_down"][i])
        if kind == 0:
            q, k, vt = _qkv_a(x, gain, p["a_w_qk"], p["a_w_vt"], p["e"], p["et"],
                              p["a_head_gain"], p["cos"], p["sin"], seq)
            x = flat(_attn_a(per_seq(x), per_seq(q), per_seq(k), vt, p["a_w_o"]))
            x = _ffn(x, *ffn_args, p["final_norm"], "plain")
        elif kind == 1:
            x = _pool(x, gain, p["b_w_pool"], p["b_scale"], seq)
            x = _ffn(x, *ffn_args, p["final_norm"], "plain")
        elif kind == 2:
            lam_init = 0.8 - 0.6 * math.exp(-0.3 * i)
            q, k, vt = _qkv_c(x, gain, p["c_w_qk"], p["c_w_vt"], seq)
            x = flat(_attn_c(per_seq(x), per_seq(q), per_seq(k), vt, p["c_lambda"],
                             p["c_sub_gain_t"], p["c_w_o"], lam_init))
            x, h = _ffn(x, *ffn_args, p["attn_norm"][i + 1][None], "emit")
        else:
            assert i == DEPTH - 1
            x = _fourier(per_seq(x), per_seq(h), p["ct"], p["st"], p["cc"], p["sc"],
                         p["rev"], p["d_w_o"]).reshape(n, D_MODEL)
            tiles = seq // TM
            x = _ffn(x, *ffn_args, p["final_norm"], "final",
                     in_tile=lambda t: (t // tiles) * tiles + _fourier_slot(t % tiles))
    return per_seq(x)


def kernel(x_prompt, x_sample, attn_norm, ffn_norm, final_norm, a_w_qkv, a_q_gain, a_k_gain,
           a_w_o, b_w_pool, b_scale, c_w_qkv, c_lambda, c_sub_gain, c_w_o, d_w_o,
           w_gate, w_up, w_down):
    seq = x_prompt.shape[1]
    assert x_sample.shape[1] == seq and seq % FFN_TM == 0 and seq % (SUB_TILES * TQ) == 0
    e, et = _head_reduce_tables()
    cos, sin = _rope_tables(seq)
    ct, st = _dft_tables(seq, seq ** -0.5, rows=seq // 2 + F_EXTRA)
    cc, sc = _dft_tables(F_GROUP_DIM, F_GROUP_DIM ** -0.5)
    wa = a_w_qkv[0]
    a_w_qk = wa[:, :A_QK_W]
    head_gain = jnp.concatenate([jnp.tile(a_q_gain[0], A_HEADS) * (A_HEAD_DIM ** -0.5 * LOG2E),
                                 jnp.tile(a_k_gain[0], A_KV_HEADS)])[None]
    wc = c_w_qkv[0]
    p = dict(
        attn_norm=attn_norm, ffn_norm=ffn_norm, final_norm=final_norm[None],
        a_w_qk=a_w_qk.astype(BF16), a_w_vt=wa[:, A_Q_W + A_V_W:].T.astype(BF16),
        a_head_gain=head_gain, a_w_o=a_w_o[0].astype(BF16),
        e=e, et=et, cos=cos, sin=sin,
        b_w_pool=b_w_pool[0].astype(BF16), b_scale=b_scale[0][None],
        c_w_qk=wc[:, :2 * D_MODEL].astype(BF16), c_w_vt=wc[:, 2 * D_MODEL:].T.astype(BF16),
        c_lambda=c_lambda[0],
        c_sub_gain_t=jnp.broadcast_to(c_sub_gain[0][:, None], (2 * C_HEAD_DIM, TQ)),
        c_w_o=c_w_o[0].astype(BF16),
        d_w_o=d_w_o[0].astype(BF16), ct=ct, st=st, cc=cc, sc=sc, rev=_reversal_matrix(),
        w_gate=w_gate.astype(BF16), w_up=w_up.astype(BF16), w_down=w_down.astype(BF16),
    )
    return _trunk(x_prompt, p), _trunk(x_sample, p)
```

```python
import functools
import math

import jax
import jax.numpy as jnp
import ml_dtypes
from jax import lax
from jax.experimental import pallas as pl
from jax.experimental.pallas import tpu as pltpu

F32 = jnp.float32
BF16 = jnp.bfloat16

D_MODEL = 1024
DEPTH = 4
LAYERS = tuple(range(DEPTH))
GRID_W = 64
NORM_EPS = 1e-6
A_HEADS = 16
A_KV_HEADS = 4
A_HEAD_DIM = 64
A_GROUP = A_HEADS // A_KV_HEADS
ROPE_THETA = 10000.0
POOL_WINDOWS = (2, 4, 8, 16)
POOL_GROUP = D_MODEL // len(POOL_WINDOWS)
POOL_HALO = 8
C_HEADS = 8
C_HEAD_DIM = 64
F_GROUPS = 4
F_GROUP_DIM = D_MODEL // F_GROUPS
D_FF = 2816
LOG2E = math.log2(math.e)


def _bf16_parts(x, n):
    parts = []
    for _ in range(n):
        parts.append(float(ml_dtypes.bfloat16(x)))
        x -= parts[-1]
    return tuple(parts)


assert C_HEADS == 8
ALIBI_E = 2.0 ** (-8.0 / C_HEADS) * LOG2E
ALIBI_E_PARTS = _bf16_parts(ALIBI_E, 3)

LANES = 128
BF16_ROWS = 16
VMEM_LIMIT = 56 * 1024 * 1024

A_Q_W = A_HEADS * A_HEAD_DIM
A_V_W = A_KV_HEADS * A_HEAD_DIM
A_QK_W = A_Q_W + A_V_W
A_KDUP_W = A_KV_HEADS * LANES

TM = 512
FFN_TM = 2 * TM
TQ = 256
SUB_TILES = 2
OUT_DELAY = 2
FF_CHUNK = 256
KEY_SPLITS = 2


def _params(*sem):
    return pltpu.CompilerParams(dimension_semantics=sem, vmem_limit_bytes=VMEM_LIMIT)


def _resident(shape):
    nd = len(shape)
    return pl.BlockSpec(shape, lambda *_: (0,) * nd, pipeline_mode=pl.Buffered(1))


def _rms(x, gain):
    ms = jnp.mean(x * x, axis=-1, keepdims=True)
    return x * lax.rsqrt(ms + NORM_EPS) * gain


def _split_bf16(x):
    hi = x.astype(BF16)
    lo = (x - hi.astype(F32)).astype(BF16)
    return hi, lo


def _dot(a, b):
    return jnp.dot(a, b, preferred_element_type=F32)


def _dot_nt(a, b):
    return lax.dot_general(a, b, (((1,), (1,)), ((), ())), preferred_element_type=F32)


def _half_lanes_mask(rows, upper):
    lane = lax.broadcasted_iota(jnp.int32, (rows, LANES), 1)
    return lane >= LANES // 2 if upper else lane < LANES // 2


def _scores_t(k_halves, q, own_bias=None, own_row=0):
    halves = [_dot_nt(k, q) for k in k_halves]
    if own_bias is not None:
        first, end = halves[0], own_row + own_bias.shape[0]
        pieces = [first[:own_row], first[own_row:end] - own_bias, first[end:]]
        halves[0] = jnp.concatenate([piece for piece in pieces if piece.shape[0]], axis=0)
    m = functools.reduce(jnp.maximum, [jnp.max(s_t, axis=0, keepdims=True) for s_t in halves])
    return halves, m


S_SLOTS = 3
P_SLOTS = 2


def _stage_scratch(keys):
    return [pltpu.VMEM((S_SLOTS, keys // 2, TQ), F32), pltpu.VMEM((P_SLOTS, keys // 2, TQ), BF16),
            pltpu.VMEM((PEND_ROWS, TQ), F32)]


PEND_ROWS = 2 * 64 + 16


def _two_block_items(score_half, value_half, finish, s_scr, p_scr, pend_scr):
    first_max = []

    def scores(j):
        halves, m = score_half(*divmod(j, 2))
        s_scr[j % S_SLOTS] = halves[0]
        return j, m

    def probs(j, m):
        p_scr[j % P_SLOTS] = jnp.exp2(s_scr[j % S_SLOTS] - m).astype(BF16)
        return m

    def values(j, m):
        i, r = divmod(j, 2)
        oa = value_half(i, r, p_scr[j % P_SLOTS])
        rows = oa.shape[0]
        if r == 0:
            pend_scr[0:rows, :] = oa
            first_max.append(m)
        else:
            m0 = first_max.pop()
            top = jnp.maximum(m0, m)
            tot = pend_scr[0:rows, :] * jnp.exp2(m0 - top) + oa * jnp.exp2(m - top)
            d = rows - BF16_ROWS
            finish(i, tot[:d] / tot[d:d + 1])

    return scores, probs, values


def _park_scores(s_scr, i, halves, m):
    part = halves[0].shape[0]
    for r, s_t in enumerate(halves):
        s_scr[i % s_scr.shape[0], r * part:(r + 1) * part, :] = s_t
    return i, m


def _probs_t(s_scr, p_scr, i, m):
    p_scr[i % p_scr.shape[0]] = jnp.exp2(s_scr[i % s_scr.shape[0]] - m).astype(BF16)
    return i


def _weighted_values_t(v_aug, p_t):
    d = v_aug.shape[0] - BF16_ROWS
    oa = _dot(v_aug, p_t)
    return oa[:d] / oa[d:d + 1]


def _pipelined(n, scores, probs, values, after=None):
    s = {0: scores(0)}
    if n > 1:
        s[1] = scores(1)
    p = {0: probs(*s.pop(0))}
    for i in range(n):
        if i + 2 < n:
            s[i + 2] = scores(i + 2)
        if i + 1 < n:
            p[i + 1] = probs(*s.pop(i + 1))
        values(i, p.pop(i))
        if after is not None:
            after(i)


def _project_late(n_sub, items_per_sub, project):
    done = []

    def after(i):
        for u in range(n_sub):
            ready = (u + 1) * items_per_sub - 1 + OUT_DELAY
            if u not in done and i >= min(ready, n_sub * items_per_sub - 1):
                done.append(u)
                project(u)
    return after


def _ffn_apply(x, g_ref, wg_ref, wu_ref, wd_ref):
    h = _rms(x, g_ref[...]).astype(BF16)
    acc = x
    for c in range(D_FF // FF_CHUNK):
        cols = slice(c * FF_CHUNK, (c + 1) * FF_CHUNK)
        g = _dot(h, wg_ref[:, cols])
        u = _dot(h, wu_ref[:, cols])
        a = (g * jax.nn.sigmoid(g) * u).astype(BF16)
        acc = acc + _dot(a, wd_ref[cols, :])
    return acc


def _ffn_specs():
    return [_resident((1, D_MODEL)), _resident((D_MODEL, D_FF)), _resident((D_MODEL, D_FF)),
            _resident((D_FF, D_MODEL))]


def _ffn_kernel(mode, x_ref, g_ref, wg_ref, wu_ref, wd_ref, ng_ref, o_ref, *h_ref):
    acc = _ffn_apply(x_ref[...], g_ref, wg_ref, wu_ref, wd_ref)
    if mode == "plain":
        o_ref[...] = acc
        return
    normed = _rms(acc, ng_ref[...])
    o_ref[...] = normed if mode == "final" else acc
    if mode == "emit":
        h_ref[0][...] = normed.astype(BF16)


def _ffn(x, gain, wg, wu, wd, next_gain, mode, in_tile=lambda i: i):
    n = x.shape[0]
    tm = TM if mode == "final" else FFN_TM
    row = pl.BlockSpec((tm, D_MODEL), lambda i: (i, 0))
    out = jax.ShapeDtypeStruct((n, D_MODEL), F32)
    emit = mode == "emit"
    return pl.pallas_call(
        functools.partial(_ffn_kernel, mode),
        grid=(n // tm,),
        in_specs=[pl.BlockSpec((tm, D_MODEL), lambda i: (in_tile(i), 0)), *_ffn_specs(),
                  _resident((1, D_MODEL))],
        out_specs=[row, row] if emit else row,
        out_shape=[out, jax.ShapeDtypeStruct((n, D_MODEL), BF16)] if emit else out,
        compiler_params=_params("parallel"),
        name="ffn_" + mode,
    )(x, gain, wg, wu, wd, next_gain)


def _qkv_a_kernel(x_ref, g_ref, w_ref, wvt_ref, e_ref, et_ref, hg_ref, cos_ref, sin_ref,
                  q_ref, k_ref, vt_ref):
    h = _rms(x_ref[...], g_ref[...]).astype(BF16)
    qk = _dot(h, w_ref[...])
    sq_hi, sq_lo = _split_bf16(qk * qk)
    ms = _dot(sq_hi, e_ref[...]) + _dot(sq_lo, e_ref[...])
    rs = _dot(jnp.concatenate(_split_bf16(lax.rsqrt(ms + NORM_EPS)), axis=1), et_ref[...])
    qn = qk * rs * hg_ref[...]
    lane = lax.broadcasted_iota(jnp.int32, qn.shape, 1)
    partner = jnp.where(lane % 2 == 0, pltpu.roll(qn, A_QK_W - 1, 1), pltpu.roll(qn, 1, 1))
    reps = A_QK_W // LANES
    cos = jnp.concatenate([cos_ref[...]] * reps, axis=1)
    sin = jnp.concatenate([sin_ref[...]] * reps, axis=1)
    out = qn * cos + partner * sin
    q_ref[...] = out[:, :A_Q_W].astype(BF16)
    kk = out[:, A_Q_W:]
    up = pltpu.roll(kk, A_HEAD_DIM, 1)
    down = pltpu.roll(kk, A_V_W - A_HEAD_DIM, 1)
    lower = _half_lanes_mask(kk.shape[0], False)
    for g in range(A_KV_HEADS):
        src = slice((g // 2) * LANES, (g // 2 + 1) * LANES)
        pair = (jnp.where(lower, kk[:, src], up[:, src]) if g % 2 == 0
                else jnp.where(lower, down[:, src], kk[:, src]))
        k_ref[:, g * LANES:(g + 1) * LANES] = pair.astype(BF16)
    vt_ref[0] = _dot_nt(wvt_ref[...], h).astype(BF16)


def _qkv_a(x, gain, w, wvt, e, et, head_gain, cos, sin, seq):
    n = x.shape[0]
    spt = seq // TM
    row = lambda width: pl.BlockSpec((TM, width), lambda i: (i, 0))
    tab = pl.BlockSpec((TM, LANES), lambda i: (i % spt, 0))
    return pl.pallas_call(
        _qkv_a_kernel,
        grid=(n // TM,),
        in_specs=[row(D_MODEL), _resident((1, D_MODEL)), _resident((D_MODEL, A_QK_W)),
                  _resident((A_V_W, D_MODEL)), _resident((A_QK_W, LANES)),
                  _resident((2 * LANES, A_QK_W)), _resident((1, A_QK_W)), tab, tab],
        out_specs=[row(A_Q_W), row(A_KDUP_W),
                   pl.BlockSpec((1, A_V_W, TM), lambda i: (i // spt, 0, i % spt))],
        out_shape=[jax.ShapeDtypeStruct((n, A_Q_W), BF16),
                   jax.ShapeDtypeStruct((n, A_KDUP_W), BF16),
                   jax.ShapeDtypeStruct((n // seq, A_V_W, seq), BF16)],
        compiler_params=_params("parallel"),
        name="qkv_a",
    )(x, gain, w, wvt, e, et, head_gain, cos, sin)


def _project_out(u, x_ref, ot_scr, wo_ref, o_ref):
    rows = slice(u * TQ, (u + 1) * TQ)
    o = ot_scr[u].T.astype(BF16)
    o_ref[0, rows] = x_ref[0, rows] + _dot(o, wo_ref[...])


def _attn_a_kernel(x_ref, q_ref, k_ref, vt_ref, wo_ref, o_ref, ot_scr, s_scr, p_scr, pend_scr):
    half = k_ref.shape[1] // 2
    hd = A_HEAD_DIM
    ones = jnp.ones((BF16_ROWS, half), BF16)

    def score_half(i, r):
        u, h = divmod(i, A_HEADS)
        g = h // A_GROUP
        q_pair = q_ref[0, u * TQ:(u + 1) * TQ, (h // 2) * LANES:(h // 2 + 1) * LANES]
        q_h = jnp.where(_half_lanes_mask(TQ, h % 2 == 1), q_pair, jnp.zeros_like(q_pair))
        return _scores_t([k_ref[0, r * half:(r + 1) * half, g * LANES:(g + 1) * LANES]], q_h)

    def value_half(i, r, p_t):
        g = (i % A_HEADS) // A_GROUP
        v_t = vt_ref[0, g * hd:(g + 1) * hd, r * half:(r + 1) * half]
        return _dot(jnp.concatenate([v_t, ones], axis=0), p_t)

    def finish(i, o):
        u, h = divmod(i, A_HEADS)
        ot_scr[u, h * hd:(h + 1) * hd, :] = o

    project = functools.partial(_project_out, x_ref=x_ref, ot_scr=ot_scr, wo_ref=wo_ref,
                                o_ref=o_ref)
    _pipelined(2 * SUB_TILES * A_HEADS,
               *_two_block_items(score_half, value_half, finish, s_scr, p_scr, pend_scr),
               _project_late(SUB_TILES, 2 * A_HEADS, project))


def _attn_a(x, q, k, vt, wo):
    b, seq, _ = x.shape
    rows = SUB_TILES * TQ
    tile = pl.BlockSpec((1, rows, D_MODEL), lambda i, j: (i, j, 0))
    return pl.pallas_call(
        _attn_a_kernel,
        grid=(b, seq // rows),
        in_specs=[tile, tile,
                  pl.BlockSpec((1, seq, A_KDUP_W), lambda i, j: (i, 0, 0)),
                  pl.BlockSpec((1, A_V_W, seq), lambda i, j: (i, 0, 0)),
                  _resident((D_MODEL, D_MODEL))],
        out_specs=tile,
        out_shape=jax.ShapeDtypeStruct(x.shape, F32),
        scratch_shapes=[pltpu.VMEM((SUB_TILES, D_MODEL, TQ), F32), *_stage_scratch(seq)],
        compiler_params=_params("parallel", "parallel"),
        name="attn_a",
    )(x, q, k, vt, wo)


def _pool_kernel(seq, x_ref, prev_ref, next_ref, g_ref, w_ref, sc_ref, o_ref, h_scr):
    i = pl.program_id(0)
    t0 = (i % (seq // TM)) * TM
    gain = g_ref[...]
    x = x_ref[...]
    h = _rms(x, gain)
    h_scr[0:POOL_HALO, :] = jnp.where(t0 > 0, _rms(prev_ref[...], gain), 0.0)
    h_scr[POOL_HALO:POOL_HALO + TM, :] = h
    h_scr[POOL_HALO + TM:, :] = jnp.where(t0 + TM < seq, _rms(next_ref[...], gain), 0.0)
    t = t0 + lax.broadcasted_iota(jnp.int32, (TM, 1), 0)
    ys = []
    for g, win in enumerate(POOL_WINDOWS):
        lanes = slice(g * POOL_GROUP, (g + 1) * POOL_GROUP)
        half = win // 2
        total = h_scr[POOL_HALO - half:POOL_HALO - half + TM, lanes]
        for d in range(1, win):
            total = total + h_scr[POOL_HALO - half + d:POOL_HALO - half + d + TM, lanes]
        lo = jnp.clip(t - half, 0, seq - 1)
        hi = jnp.clip(t + half - 1, 0, seq - 1)
        cnt = (hi - lo + 1).astype(F32)
        p = (total / cnt - h[:, lanes]).astype(BF16)
        ys.append(_dot(p, w_ref[g]))
    o_ref[...] = x + jnp.concatenate(ys, axis=1) * sc_ref[...]


def _pool(x, gain, w, scale, seq):
    n = x.shape[0]
    halo_per_tile = TM // POOL_HALO
    last_halo = n // POOL_HALO - 1
    row = pl.BlockSpec((TM, D_MODEL), lambda i: (i, 0))
    prev = pl.BlockSpec((POOL_HALO, D_MODEL),
                        lambda i: (jnp.maximum(i * halo_per_tile - 1, 0), 0))
    nxt = pl.BlockSpec((POOL_HALO, D_MODEL),
                       lambda i: (jnp.minimum((i + 1) * halo_per_tile, last_halo), 0))
    ng = len(POOL_WINDOWS)
    return pl.pallas_call(
        functools.partial(_pool_kernel, seq),
        grid=(n // TM,),
        in_specs=[row, prev, nxt, _resident((1, D_MODEL)),
                  _resident((ng, POOL_GROUP, POOL_GROUP)), _resident((1, D_MODEL))],
        out_specs=row,
        out_shape=jax.ShapeDtypeStruct((n, D_MODEL), F32),
        scratch_shapes=[pltpu.VMEM((TM + 2 * POOL_HALO, D_MODEL), F32)],
        compiler_params=_params("parallel"),
        name="pool",
    )(x, x, x, gain, w, scale)


def _qkv_c_kernel(x_ref, g_ref, w_ref, wvt_ref, q_ref, k_ref, vt_ref):
    h = _rms(x_ref[...], g_ref[...]).astype(BF16)
    width = 2 * C_HEADS * C_HEAD_DIM
    qk = _dot(h, w_ref[...])
    q_ref[...] = (qk[:, :width] * (C_HEAD_DIM ** -0.5 * LOG2E)).astype(BF16)
    k_ref[...] = qk[:, width:].astype(BF16)
    vt_ref[0] = _dot_nt(wvt_ref[...], h).astype(BF16)


def _qkv_c(x, gain, w, wvt, seq):
    n = x.shape[0]
    spt = seq // TM
    row = pl.BlockSpec((TM, D_MODEL), lambda i: (i, 0))
    out = jax.ShapeDtypeStruct((n, D_MODEL), BF16)
    return pl.pallas_call(
        _qkv_c_kernel,
        grid=(n // TM,),
        in_specs=[row, _resident((1, D_MODEL)), _resident((D_MODEL, 2 * D_MODEL)),
                  _resident((D_MODEL, D_MODEL))],
        out_specs=[row, row, pl.BlockSpec((1, D_MODEL, TM), lambda i: (i // spt, 0, i % spt))],
        out_shape=[out, out, jax.ShapeDtypeStruct((n // seq, D_MODEL, seq), BF16)],
        compiler_params=_params("parallel"),
        name="qkv_c",
    )(x, gain, w, wvt)


def _lane_pattern(lane, values):
    out = jnp.zeros(lane.shape, F32)
    for l, v in enumerate(values):
        out = jnp.where(lane == l, v, out)
    return out


def _attn_c_kernel(lam_init, nb, x_ref, q_ref, *refs):
    k_refs, vt_refs = refs[:nb], refs[nb:2 * nb]
    (lp_ref, sg_ref, wo_ref, o_ref, ot_scr, s_scr, p_scr, pend_scr, kaug_scr, qaug_scr,
     bias_scr) = refs[2 * nb:]
    hw = 2 * C_HEAD_DIM
    first = pl.program_id(1) * SUB_TILES
    lp = lp_ref[...]
    lam = (jnp.exp(jnp.sum(lp[0:1] * lp[1:2], axis=-1, keepdims=True))
           - jnp.exp(jnp.sum(lp[2:3] * lp[3:4], axis=-1, keepdims=True)) + lam_init)

    lane = lax.broadcasted_iota(jnp.int32, (TQ, LANES), 1)
    offset = lax.broadcasted_iota(jnp.int32, (TQ, LANES), 0).astype(F32)
    e_parts = list(ALIBI_E_PARTS)
    for d in range(nb):
        j0 = (((first + d) % nb) * TQ).astype(F32)
        pattern = _lane_pattern(lane, 2 * e_parts + 3 * [j0] + 3 * [offset])
        wrapped = first + d >= nb
        for u in range(SUB_TILES):
            if d == u:
                kaug_scr[u, d] = jnp.zeros((TQ, LANES), BF16)
            else:
                sigma = 1.0 if d < u else jnp.where(wrapped, 1.0, -1.0)
                kaug_scr[u, d] = (sigma * pattern).astype(BF16)
    for u in range(SUB_TILES):
        i0 = ((first + u) * TQ).astype(F32)
        q_aug0 = _lane_pattern(lane, 3 * [-offset] + 3 * [-i0] + 2 * e_parts)
        for h in range(C_HEADS):
            qaug_scr[u, h] = (q_aug0 * 2.0 ** -h).astype(BF16)
    own_dist = jnp.abs(lax.broadcasted_iota(jnp.int32, (TQ, TQ), 0)
                       - lax.broadcasted_iota(jnp.int32, (TQ, TQ), 1)).astype(F32)
    for h in range(C_HEADS):
        bias_scr[h] = own_dist * (ALIBI_E * 2.0 ** -h)
    per_half = nb // 2
    ones = jnp.ones((BF16_ROWS, per_half * TQ), BF16)
    assert SUB_TILES <= per_half

    def score_half(i, r):
        u, rest = divmod(i, 2 * C_HEADS)
        h, c = divmod(rest, 2)
        lanes = slice(h * hw, (h + 1) * hw)
        q_pair = q_ref[0, u * TQ:(u + 1) * TQ, lanes]
        q_c = jnp.where(_half_lanes_mask(TQ, c == 1), q_pair, jnp.zeros_like(q_pair))
        q_full = jnp.concatenate([q_c, qaug_scr[u, h]], axis=1)
        k_half = jnp.concatenate(
            [jnp.concatenate([k_refs[d][0, :, lanes], kaug_scr[u, d]], axis=1)
             for d in range(r * per_half, (r + 1) * per_half)], axis=0)
        return _scores_t([k_half], q_full, bias_scr[h] if r == 0 else None, u * TQ)

    def value_half(i, r, p_t):
        h = (i % (2 * C_HEADS)) // 2
        v_t = jnp.concatenate([vt_refs[d][0, h * hw:(h + 1) * hw, :]
                               for d in range(r * per_half, (r + 1) * per_half)], axis=1)
        return _dot(jnp.concatenate([v_t, ones], axis=0), p_t)

    def finish(i, o_c):
        u, rest = divmod(i, 2 * C_HEADS)
        h, c = divmod(rest, 2)
        rows = slice(h * hw, (h + 1) * hw)
        if c == 0:
            ot_scr[u, rows, :] = o_c
        else:
            o = ot_scr[u, rows, :] - lam * o_c
            ms = jnp.mean(o * o, axis=0, keepdims=True)
            ot_scr[u, rows, :] = o * lax.rsqrt(ms + NORM_EPS) * sg_ref[...] * (1.0 - lam_init)

    project = functools.partial(_project_out, x_ref=x_ref, ot_scr=ot_scr, wo_ref=wo_ref,
                                o_ref=o_ref)
    _pipelined(2 * SUB_TILES * 2 * C_HEADS,
               *_two_block_items(score_half, value_half, finish, s_scr, p_scr, pend_scr),
               _project_late(SUB_TILES, 4 * C_HEADS, project))


def _attn_c(x, q, k, vt, lam_params, sub_gain_t, wo, lam_init):
    b, seq, _ = x.shape
    nb = seq // TQ
    rows = SUB_TILES * TQ
    tile = pl.BlockSpec((1, rows, D_MODEL), lambda i, j: (i, j, 0))
    k_blocks = [pl.BlockSpec((1, TQ, D_MODEL),
                             lambda i, j, d=d: (i, (j * SUB_TILES + d) % nb, 0))
                for d in range(nb)]
    vt_blocks = [pl.BlockSpec((1, D_MODEL, TQ),
                              lambda i, j, d=d: (i, 0, (j * SUB_TILES + d) % nb))
                 for d in range(nb)]
    return pl.pallas_call(
        functools.partial(_attn_c_kernel, lam_init, nb),
        grid=(b, seq // rows),
        in_specs=[tile, tile, *k_blocks, *vt_blocks,
                  _resident((4, C_HEAD_DIM)), _resident((2 * C_HEAD_DIM, TQ)),
                  _resident((D_MODEL, D_MODEL))],
        out_specs=tile,
        out_shape=jax.ShapeDtypeStruct(x.shape, F32),
        scratch_shapes=[pltpu.VMEM((SUB_TILES, D_MODEL, TQ), F32), *_stage_scratch(seq),
                        pltpu.VMEM((SUB_TILES, nb, TQ, LANES), BF16),
                        pltpu.VMEM((SUB_TILES, C_HEADS, TQ, LANES), BF16),
                        pltpu.VMEM((C_HEADS, TQ, TQ), F32)],
        compiler_params=_params("parallel", "parallel"),
        name="attn_c",
    )(x, q, *([k] * nb), *([vt] * nb), lam_params, sub_gain_t, wo)


F_EXTRA = BF16_ROWS


def _fourier_kernel(xd_ref, xm_ref, h_ref, ct_ref, st_ref, ctx_ref, stx_ref, cc_ref, sc_ref,
                    rev_ref, wo_ref, o_ref):
    h = h_ref[0]
    ct = jnp.concatenate([ct_ref[...], ctx_ref[...]], axis=0)
    st = jnp.concatenate([st_ref[...], stx_ref[...]], axis=0)
    p = _dot(ct, h).astype(BF16)
    q = _dot(st, h).astype(BF16)
    direct, mirror = [], []
    for g in range(F_GROUPS):
        lanes = slice(g * F_GROUP_DIM, (g + 1) * F_GROUP_DIM)
        a = _dot(p[:, lanes], cc_ref[...])
        b = _dot(q[:, lanes], sc_ref[...])
        direct.append(a[:TM] - b[:TM])
        mirror.append(a + b)
    f = jnp.concatenate(direct, axis=1).astype(BF16)
    o_ref[0, 0, 0] = xd_ref[0, 0] + _dot(f, wo_ref[...])
    fm = jnp.concatenate(mirror, axis=1).astype(BF16)
    rev = _dot(rev_ref[...], fm[:TM])
    row = lax.broadcasted_iota(jnp.int32, (TM, D_MODEL), 0)
    fm_rev = jnp.where(row == 0, fm[TM:TM + 1].astype(F32), rev).astype(BF16)
    o_ref[0, 0, 1] = xm_ref[0, 0] + _dot(fm_rev, wo_ref[...])


def _fourier(x, h, ct, st, cc, sc, rev, wo):
    b, seq, _ = x.shape
    nt = seq // TM
    assert nt == 4
    x4 = x.reshape(b, nt, TM, D_MODEL)
    tile = lambda f: pl.BlockSpec((1, 1, TM, D_MODEL), lambda i, j: (i, f(j), 0, 0))
    whole = pl.BlockSpec((1, seq, D_MODEL), lambda i, j: (i, 0, 0))
    tab = pl.BlockSpec((TM, seq), lambda i, j: (j, 0))
    tab_extra = pl.BlockSpec((F_EXTRA, seq), lambda i, j: ((j + 1) * (TM // F_EXTRA), 0))
    square = _resident((F_GROUP_DIM, F_GROUP_DIM))
    return pl.pallas_call(
        _fourier_kernel,
        grid=(b, nt // 2),
        in_specs=[tile(lambda j: j), tile(lambda j: nt - 1 - j), whole, tab, tab, tab_extra,
                  tab_extra, square, square, _resident((TM, TM)), _resident((D_MODEL, D_MODEL))],
        out_specs=pl.BlockSpec((1, 1, 2, TM, D_MODEL), lambda i, j: (i, j, 0, 0, 0)),
        out_shape=jax.ShapeDtypeStruct((b, nt // 2, 2, TM, D_MODEL), F32),
        compiler_params=_params("parallel", "parallel"),
        name="fourier",
    )(x4, x4, h, ct, st, ct, st, cc, sc, rev, wo)


def _fourier_slot(t):
    return jnp.where(t < 2, 2 * t, 7 - 2 * t)


def _rope_tables(seq):
    rows = seq // GRID_W
    r, c = jnp.meshgrid(jnp.arange(rows), jnp.arange(GRID_W), indexing="ij")
    r = r.reshape(-1).astype(F32)
    c = c.reshape(-1).astype(F32)
    n = A_HEAD_DIM // 4
    freqs = ROPE_THETA ** (-jnp.arange(n, dtype=F32) / n)
    ang = jnp.concatenate([r[:, None] * freqs, c[:, None] * freqs], axis=-1)
    cos = jnp.repeat(jnp.cos(ang), 2, axis=-1)
    sin = jnp.repeat(jnp.sin(ang), 2, axis=-1)
    sign = jnp.where(jnp.arange(A_HEAD_DIM) % 2 == 0, -1.0, 1.0).astype(F32)
    sin = sin * sign
    reps = LANES // A_HEAD_DIM
    return jnp.tile(cos, (1, reps)), jnp.tile(sin, (1, reps))


def _dft_tables(n, scale, rows=None):
    k = jnp.arange(rows or n, dtype=jnp.int32)
    kn = (k[:, None] * jnp.arange(n, dtype=jnp.int32)[None, :]) % n
    ang = kn.astype(F32) * (2.0 * math.pi / n)
    return (jnp.cos(ang) * scale).astype(BF16), (jnp.sin(ang) * scale).astype(BF16)


def _reversal_matrix():
    r = jnp.arange(TM)
    return (r[:, None] + r[None, :] == TM).astype(BF16)


def _head_reduce_tables():
    lane_head = jnp.arange(A_QK_W) // A_HEAD_DIM
    onehot = (lane_head[:, None] == jnp.arange(LANES)[None, :]).astype(F32)
    expand = onehot.T.astype(BF16)
    return (onehot / A_HEAD_DIM).astype(BF16), jnp.concatenate([expand, expand], axis=0)


def _trunk(x, p):
    b, seq, _ = x.shape
    n = b * seq
    flat = lambda a: a.reshape(n, a.shape[-1])
    per_seq = lambda a: a.reshape(b, seq, a.shape[-1])
    x = flat(x)
    for i in LAYERS:
        kind = i % 4
        gain = p["attn_norm"][i][None]
        ffn_args = (p["ffn_norm"][i][None], p["w_gate"][i], p["w_up"][i], p["w_down"][i])
        if kind == 0:
            q, k, vt = _qkv_a(x, gain, p["a_w_qk"], p["a_w_vt"], p["e"], p["et"],
                              p["a_head_gain"], p["cos"], p["sin"], seq)
            x = flat(_attn_a(per_seq(x), per_seq(q), per_seq(k), vt, p["a_w_o"]))
            x = _ffn(x, *ffn_args, p["final_norm"], "plain")
        elif kind == 1:
            x = _pool(x, gain, p["b_w_pool"], p["b_scale"], seq)
            x = _ffn(x, *ffn_args, p["final_norm"], "plain")
        elif kind == 2:
            lam_init = 0.8 - 0.6 * math.exp(-0.3 * i)
            q, k, vt = _qkv_c(x, gain, p["c_w_qk"], p["c_w_vt"], seq)
            x = flat(_attn_c(per_seq(x), per_seq(q), per_seq(k), vt, p["c_lambda"],
                             p["c_sub_gain_t"], p["c_w_o"], lam_init))
            x, h = _ffn(x, *ffn_args, p["attn_norm"][i + 1][None], "emit")
        else:
            assert i == DEPTH - 1
            x = _fourier(per_seq(x), per_seq(h), p["ct"], p["st"], p["cc"], p["sc"],
                         p["rev"], p["d_w_o"]).reshape(n, D_MODEL)
            tiles = seq // TM
            x = _ffn(x, *ffn_args, p["final_norm"], "final",
                     in_tile=lambda t: (t // tiles) * tiles + _fourier_slot(t % tiles))
    return per_seq(x)


def kernel(x_prompt, x_sample, attn_norm, ffn_norm, final_norm, a_w_qkv, a_q_gain, a_k_gain,
           a_w_o, b_w_pool, b_scale, c_w_qkv, c_lambda, c_sub_gain, c_w_o, d_w_o,
           w_gate, w_up, w_down):
    seq = x_prompt.shape[1]
    assert x_sample.shape[1] == seq and seq % FFN_TM == 0 and seq % (SUB_TILES * TQ) == 0
    e, et = _head_reduce_tables()
    cos, sin = _rope_tables(seq)
    ct, st = _dft_tables(seq, seq ** -0.5, rows=seq // 2 + F_EXTRA)
    cc, sc = _dft_tables(F_GROUP_DIM, F_GROUP_DIM ** -0.5)
    wa = a_w_qkv[0]
    a_w_qk = wa[:, :A_QK_W]
    head_gain = jnp.concatenate([jnp.tile(a_q_gain[0], A_HEADS) * (A_HEAD_DIM ** -0.5 * LOG2E),
                                 jnp.tile(a_k_gain[0], A_KV_HEADS)])[None]
    wc = c_w_qkv[0]
    p = dict(
        attn_norm=attn_norm, ffn_norm=ffn_norm, final_norm=final_norm[None],
        a_w_qk=a_w_qk.astype(BF16), a_w_vt=wa[:, A_Q_W + A_V_W:].T.astype(BF16),
        a_head_gain=head_gain, a_w_o=a_w_o[0].astype(BF16),
        e=e, et=et, cos=cos, sin=sin,
        b_w_pool=b_w_pool[0].astype(BF16), b_scale=b_scale[0][None],
        c_w_qk=wc[:, :2 * D_MODEL].astype(BF16), c_w_vt=wc[:, 2 * D_MODEL:].T.astype(BF16),
        c_lambda=c_lambda[0],
        c_sub_gain_t=jnp.broadcast_to(c_sub_gain[0][:, None], (2 * C_HEAD_DIM, TQ)),
        c_w_o=c_w_o[0].astype(BF16),
        d_w_o=d_w_o[0].astype(BF16), ct=ct, st=st, cc=cc, sc=sc, rev=_reversal_matrix(),
        w_gate=w_gate.astype(BF16), w_up=w_up.astype(BF16), w_down=w_down.astype(BF16),
    )
    return _trunk(x_prompt, p), _trunk(x_sample, p)
```
